```python
import math
import jax
import jax.numpy as jnp
from jax import lax
import numpy as np

D_MODEL = 2048
BATCH = 4
SEQ = 8192
DEPTH = 4

GRID_W = 64
CTX_LEN = 256
N_BRANCH = 4
BRANCH_W = D_MODEL // 4
HEAD_DIM = 64
A_HEADS = BRANCH_W // HEAD_DIM
A_KV = A_HEADS // 4
WINDOW = 128
Q_BLOCK = 128
B_HEADS = BRANCH_W // HEAD_DIM
B_LORA = 64
B_DECAY_SCALE = 0.606531
B_GN_EPS = 64e-5
C_HEADS = BRANCH_W // HEAD_DIM
C_KV = C_HEADS // 4
D_HD = 128
D_HEADS = BRANCH_W // D_HD
D_CONV = 5
D_CHUNK = 64
GATE_RANK = D_MODEL // 8
ROPE_THETA = 10000.0
EPS = 1e-6
NEG = -1e30
F32 = jnp.float32

W_A = 2 * BRANCH_W + 2 * A_KV * HEAD_DIM
W_B = 4 * BRANCH_W + 2 * B_LORA
W_C = 2 * BRANCH_W + 2 * C_KV * HEAD_DIM
W_DM = 4 * BRANCH_W + 4 * D_HEADS
OFF_A = 0
OFF_B = OFF_A + W_A
OFF_C = OFF_B + W_B
OFF_D = OFF_C + W_C
OFF_G = OFF_D + W_DM
PROJ_W = OFF_G + GATE_RANK

kernel_name = 'hybrid_parallel_mixer_dit'


def _rms(x, g, eps=EPS):
    xf = x.astype(F32)
    y = xf * lax.rsqrt(jnp.mean(xf * xf, axis=-1, keepdims=True) + eps)
    return (y * g.astype(F32)).astype(x.dtype)


def _l2n(x, eps=EPS):
    xf = x.astype(F32)
    return xf * lax.rsqrt(jnp.maximum(jnp.sum(xf * xf, axis=-1, keepdims=True), eps * eps))


def _heads(t, hd):
    return t.reshape(t.shape[:-1] + (t.shape[-1] // hd, hd))


def _rope_tables(rows):
    row = jnp.repeat(jnp.arange(rows, dtype=F32), GRID_W)
    col = jnp.tile(jnp.arange(GRID_W, dtype=F32), rows)
    half = HEAD_DIM // 2
    inv = ROPE_THETA ** (-jnp.arange(0, half, 2, dtype=F32) / half)
    ar = row[:, None] * inv
    ac = col[:, None] * inv
    return (jnp.cos(ar), jnp.sin(ar), jnp.cos(ac), jnp.sin(ac))


def _rot_half(x, cos, sin):
    x1, x2 = jnp.split(x, 2, axis=-1)
    cos = cos[:, None, :]
    sin = sin[:, None, :]
    return jnp.concatenate([x1 * cos - x2 * sin, x1 * sin + x2 * cos], axis=-1)


def _rope2d(x, rope):
    cr, sr, cc, sc = rope
    xr, xcol = jnp.split(x.astype(F32), 2, axis=-1)
    return jnp.concatenate([_rot_half(xr, cr, sr), _rot_half(xcol, cc, sc)], axis=-1).astype(x.dtype)


def _attn_split(p, n_kv):
    kvw = n_kv * HEAD_DIM
    q = _heads(p[..., :BRANCH_W], HEAD_DIM)
    k = _heads(p[..., BRANCH_W:BRANCH_W + kvw], HEAD_DIM)
    v = _heads(p[..., BRANCH_W + kvw:BRANCH_W + 2 * kvw], HEAD_DIM)
    return q, k, v, p[..., BRANCH_W + 2 * kvw:]


def _dense_attn(q, k, v, n_kv, sink=None):
    b, m, h, hd = q.shape
    grp = h // n_kv
    ns = k.shape[1]
    s = jnp.einsum('bqkgd,bskd->bkgqs', q.reshape(b, m, n_kv, grp, hd), k).astype(F32) * (hd ** -0.5)
    if sink is not None:
        snk = jnp.broadcast_to(sink.reshape(n_kv, grp, 1, 1).astype(F32), s.shape[:-1] + (1,))
        s = jnp.concatenate([s, snk], axis=-1)
    pr = jax.nn.softmax(s, axis=-1)[..., :ns].astype(v.dtype)
    return jnp.einsum('bkgqs,bskd->bqkgd', pr, v).reshape(b, m, h * hd)


def _mixer_a(pa, pac, sink, rope, need_ctx):
    q, k, v, g = _attn_split(pa, A_KV)
    qc, kc, vc, gc = _attn_split(pac, A_KV)
    q = _rope2d(q, rope)
    k = _rope2d(k, rope)
    b, n = q.shape[:2]
    m = kc.shape[1]
    grp = A_HEADS // A_KV
    nb = n // Q_BLOCK
    span = Q_BLOCK + 2 * WINDOW
    scale = HEAD_DIM ** -0.5
    pad = ((0, 0), (WINDOW, WINDOW), (0, 0), (0, 0))
    kp = jnp.pad(k, pad)
    vp = jnp.pad(v, pad)
    snk = sink.reshape(A_KV, grp, 1, 1).astype(F32)
    qb = jnp.swapaxes(q.reshape(b, nb, Q_BLOCK, A_KV, grp, HEAD_DIM), 0, 1)

    def block(args):
        i, qi = args
        start = i * Q_BLOCK
        ki = lax.dynamic_slice_in_dim(kp, start, span, axis=1)
        vi = lax.dynamic_slice_in_dim(vp, start, span, axis=1)
        kpos = start - WINDOW + jnp.arange(span)
        qpos = start + jnp.arange(Q_BLOCK)
        ok = (jnp.abs(qpos[:, None] - kpos[None, :]) <= WINDOW) & (kpos >= 0) & (kpos < n)
        s_loc = jnp.einsum('bqkgd,bskd->bkgqs', qi, ki).astype(F32) * scale
        s_loc = jnp.where(ok, s_loc, NEG)
        s_ctx = jnp.einsum('bqkgd,bskd->bkgqs', qi, kc).astype(F32) * scale
        s_snk = jnp.broadcast_to(snk, s_ctx.shape[:-1] + (1,))
        pr = jax.nn.softmax(jnp.concatenate([s_loc, s_ctx, s_snk], axis=-1), axis=-1).astype(v.dtype)
        o = (jnp.einsum('bkgqs,bskd->bqkgd', pr[..., :span], vi)
             + jnp.einsum('bkgqs,bskd->bqkgd', pr[..., span:span + m], vc))
        return o.reshape(o.shape[0], Q_BLOCK, BRANCH_W)

    o = lax.map(block, (jnp.arange(nb), qb))
    y = jnp.swapaxes(o, 0, 1).reshape(b, n, BRANCH_W) * jax.nn.silu(g)
    yc = _dense_attn(qc, kc, vc, A_KV, sink) * jax.nn.silu(gc) if need_ctx else None
    return y, yc


def _mixer_c(pc_, pcc, qn, kn, rope, need_ctx):
    q, k, v, g = _attn_split(pc_, C_KV)
    qc, kc, vc, gc = _attn_split(pcc, C_KV)
    q = _rope2d(_rms(q, qn), rope)
    k = _rope2d(_rms(k, kn), rope)
    qc = _rms(qc, qn)
    kc = _rms(kc, kn)
    b, n = q.shape[:2]
    nb = n // Q_BLOCK
    k_all = jnp.concatenate([k, kc], axis=1)
    v_all = jnp.concatenate([v, vc], axis=1)
    qb = jnp.swapaxes(q.reshape(b, nb, Q_BLOCK, C_HEADS, HEAD_DIM), 0, 1)
    o = lax.map(lambda qi: _dense_attn(qi, k_all, v_all, C_KV), qb)
    y = jnp.swapaxes(o, 0, 1).reshape(b, n, BRANCH_W) * jax.nn.silu(g)
    yc = _dense_attn(qc, kc, vc, C_KV) * jax.nn.silu(gc) if need_ctx else None
    return y, yc


def _token_shift(z, mu):
    prev = jnp.pad(z[:, :-1], ((0, 0), (1, 0), (0, 0)))
    nxt = jnp.pad(z[:, 1:], ((0, 0), (0, 1), (0, 0)))
    return z + mu[0] * (prev - z) + mu[1] * (nxt - z)


def _rwkv_feats(pb, mu, w0, wup, a0, aup, k_k, k_a):
    W, L = BRANCH_W, B_LORA
    z = _token_shift(pb[..., :3 * W + 2 * L], mu)
    r, k, v = z[..., :W], z[..., W:2 * W], z[..., 2 * W:3 * W]
    wl = jnp.tanh(z[..., 3 * W:3 * W + L])
    al = z[..., 3 * W + L:]
    kk = _l2n(_heads(k * k_k, HEAD_DIM))
    dirs = []
    for d in range(2):
        w = jnp.exp(-B_DECAY_SCALE * jax.nn.sigmoid((w0[d] + wl @ wup[d]).astype(F32)))
        a = jax.nn.sigmoid((a0[d] + al @ aup[d]).astype(F32))
        kt = k.astype(F32) * (1.0 + (a - 1.0) * k_a)
        dirs.append((_heads(w, HEAD_DIM), _heads(a, HEAD_DIM), _heads(kt, HEAD_DIM)))
    r = _heads(r, HEAD_DIM).astype(F32)
    v = _heads(v, HEAD_DIM).astype(F32)
    return r, v, kk, pb[..., 3 * W + 2 * L:], dirs


def _rwkv_scan(s0, r, w, k, v, kk, a, reverse):
    xs = tuple(jnp.moveaxis(t.astype(F32), 1, 0) for t in (r, w, k, v, kk, a))

    def step(s, inp):
        r_t, w_t, k_t, v_t, kk_t, a_t = inp
        sa = jnp.einsum('bhij,bhj->bhi', s, -kk_t)
        s = (s * w_t[:, :, None, :] + sa[..., None] * (kk_t * a_t)[:, :, None, :]
             + v_t[..., None] * k_t[:, :, None, :])
        return s, jnp.einsum('bhij,bhj->bhi', s, r_t)

    s, y = lax.scan(step, s0, xs, reverse=reverse)
    return s, jnp.moveaxis(y, 0, 1)


def _rwkv_out(wkv, bonus, g, ln_g, ln_b):
    b, n = wkv.shape[:2]
    mu = jnp.mean(wkv, axis=-1, keepdims=True)
    var = jnp.mean(jnp.square(wkv - mu), axis=-1, keepdims=True)
    gn = ((wkv - mu) * lax.rsqrt(var + B_GN_EPS)).reshape(b, n, BRANCH_W) * ln_g + ln_b
    return ((gn + bonus.reshape(b, n, BRANCH_W)) * jax.nn.silu(g.astype(F32))).astype(g.dtype)


def _mixer_b(pb, pbc, mu, w0, wup, a0, aup, k_k, k_a, r_k, ln_g, ln_b, need_ctx):
    r, v, kk, g, dirs = _rwkv_feats(pb, mu, w0, wup, a0, aup, k_k, k_a)
    rc, vc, kkc, gc, dirsc = _rwkv_feats(pbc, mu, w0, wup, a0, aup, k_k, k_a)
    s0 = jnp.zeros((pb.shape[0], B_HEADS, HEAD_DIM, HEAD_DIM), F32)
    rk = r_k.astype(F32)
    ys, bon, ycs, bonc = [], [], [], []
    for d in range(2):
        w, a, kt = dirs[d]
        wc, ac, ktc = dirsc[d]
        sc, yc_d = _rwkv_scan(s0, rc, wc, ktc, vc, kkc, ac, d == 1)
        _, y_d = _rwkv_scan(sc, r, w, kt, v, kk, a, d == 1)
        ys.append(y_d)
        bon.append(jnp.sum(r * kt * rk, axis=-1, keepdims=True) * v)
        if need_ctx:
            ycs.append(yc_d)
            bonc.append(jnp.sum(rc * ktc * rk, axis=-1, keepdims=True) * vc)
    y = _rwkv_out(ys[0] + ys[1], bon[0] + bon[1], g, ln_g, ln_b)
    yc = _rwkv_out(ycs[0] + ycs[1], bonc[0] + bonc[1], gc, ln_g, ln_b) if need_ctx else None
    return y, yc


def _short_conv(z, w):
    return lax.conv_general_dilated(z, w[:, None, :].astype(z.dtype), window_strides=(1,),
                                    padding=[(D_CONV // 2, D_CONV // 2)],
                                    dimension_numbers=('NWC', 'WIO', 'NWC'),
                                    feature_group_count=z.shape[-1])


def _gdn_feats(pd, conv_w, alog, dtb):
    W, H = BRANCH_W, D_HEADS
    qkv = jax.nn.silu(_short_conv(pd[..., :3 * W], conv_w)).astype(F32)
    q = _l2n(_heads(qkv[..., :W], D_HD)) * (D_HD ** -0.5)
    k = _l2n(_heads(qkv[..., W:2 * W], D_HD))
    v = _heads(qkv[..., 2 * W:], D_HD)
    ab = pd[..., 3 * W:3 * W + 4 * H].astype(F32)
    dirs = []
    for d in range(2):
        a_raw = ab[..., 2 * d * H:(2 * d + 1) * H]
        b_raw = ab[..., (2 * d + 1) * H:(2 * d + 2) * H]
        lg = -jnp.exp(alog[d].astype(F32)) * jax.nn.softplus(a_raw + dtb[d])
        dirs.append((lg, jax.nn.sigmoid(b_raw)))
    return q, k, v, pd[..., 3 * W + 4 * H:], dirs


def _gdn_chunked(s0, q, k, v, lg, beta):
    b, n, h, dk = k.shape
    dv = v.shape[-1]
    c = D_CHUNK
    nc = n // c

    def blk(t):
        return jnp.moveaxis(t.reshape((b, nc, c) + t.shape[2:]), 2, 3)

    q, k, v, lg, beta = blk(q), blk(k), blk(v), blk(lg), blk(beta)
    gcum = jnp.cumsum(lg, axis=-1)
    incl = jnp.tril(jnp.ones((c, c), dtype=bool))
    strict = jnp.tril(jnp.ones((c, c), dtype=bool), -1)
    diff = gcum[..., :, None] - gcum[..., None, :]
    dmask = jnp.where(incl, jnp.exp(jnp.where(incl, diff, 0.0)), 0.0)
    kb = k * beta[..., None]
    a_mat = jnp.where(strict, jnp.einsum('bnhid,bnhjd->bnhij', kb, k) * dmask, 0.0)
    t_mat = a_mat + jnp.eye(c, dtype=a_mat.dtype)
    rhs = jnp.concatenate([v * beta[..., None], kb * jnp.exp(gcum)[..., None]], axis=-1)
    sol = lax.linalg.triangular_solve(t_mat, rhs, left_side=True, lower=True, unit_diagonal=True)
    u, wk = sol[..., :dv], sol[..., dv:]
    qk = jnp.einsum('bnhid,bnhjd->bnhij', q, k) * dmask
    qg = q * jnp.exp(gcum)[..., None]
    glast = gcum[..., -1]
    kd = k * jnp.exp(glast[..., None] - gcum)[..., None]
    xs = tuple(jnp.moveaxis(t, 1, 0) for t in (qg, qk, u, wk, kd, jnp.exp(glast)))

    def step(s, inp):
        qg_c, qk_c, u_c, wk_c, kd_c, dec = inp
        vnew = u_c - jnp.einsum('bhcd,bhdv->bhcv', wk_c, s)
        o = jnp.einsum('bhcd,bhdv->bhcv', qg_c, s) + jnp.einsum('bhij,bhjv->bhiv', qk_c, vnew)
        s = s * dec[..., None, None] + jnp.einsum('bhcd,bhcv->bhdv', kd_c, vnew)
        return s, o

    s, o = lax.scan(step, s0, xs)
    o = jnp.moveaxis(jnp.moveaxis(o, 0, 1), 2, 3).reshape(b, n, h, dv)
    return s, o


def _gdn_dir(s0, q, k, v, lg, beta, rev):
    if rev:
        s, o = _gdn_chunked(s0, jnp.flip(q, 1), jnp.flip(k, 1), jnp.flip(v, 1),
                            jnp.flip(lg, 1), jnp.flip(beta, 1))
        return s, jnp.flip(o, 1)
    return _gdn_chunked(s0, q, k, v, lg, beta)


def _mixer_d(pd, pdc, conv_w, alog, dtb, norm_g, need_ctx):
    q, k, v, g, dirs = _gdn_feats(pd, conv_w, alog, dtb)
    qc, kc, vc, gc, dirsc = _gdn_feats(pdc, conv_w, alog, dtb)
    b, n = pd.shape[:2]
    m = pdc.shape[1]
    s0 = jnp.zeros((b, D_HEADS, D_HD, D_HD), F32)
    os_, ocs = [], []
    for d in range(2):
        sc, oc_d = _gdn_dir(s0, qc, kc, vc, dirsc[d][0], dirsc[d][1], d == 1)
        _, o_d = _gdn_dir(sc, q, k, v, dirs[d][0], dirs[d][1], d == 1)
        os_.append(o_d)
        ocs.append(oc_d)
    y = (_rms(os_[0] + os_[1], norm_g).reshape(b, n, BRANCH_W) * jax.nn.silu(g.astype(F32))).astype(g.dtype)
    yc = None
    if need_ctx:
        yc = (_rms(ocs[0] + ocs[1], norm_g).reshape(b, m, BRANCH_W) * jax.nn.silu(gc.astype(F32))).astype(gc.dtype)
    return y, yc


def _merge(ys, pm, g_up, g_b, w_br, w_out):
    acc = jax.nn.sigmoid(pm @ g_up[0] + g_b[0]) * (ys[0] @ w_br[0])
    for i in range(1, N_BRANCH):
        acc = acc + jax.nn.sigmoid(pm @ g_up[i] + g_b[i]) * (ys[i] @ w_br[i])
    return acc @ w_out


def _layer(x, xc, c, c_ctx, rope, need_ctx, norm_g, w_mod, b_mod, w_in, a_sink,
           b_mu, b_w0, b_wup, b_a0, b_aup, b_kk, b_ka, b_rk, b_lng, b_lnb,
           c_qn, c_kn, d_conv, d_alog, d_dtb, d_norm, g_up, g_b, w_br, w_out):
    shift, scale, gate = jnp.split(jax.nn.silu(c) @ w_mod + b_mod, 3, axis=-1)
    sh_c, sc_c, gt_c = jnp.split(jax.nn.silu(c_ctx) @ w_mod + b_mod, 3, axis=-1)
    h = _rms(x, norm_g) * (1.0 + scale[:, None]) + shift[:, None]
    hc = _rms(xc, norm_g) * (1.0 + sc_c) + sh_c
    p = h @ w_in
    pc = hc @ w_in
    y_a, z_a = _mixer_a(p[..., OFF_A:OFF_B], pc[..., OFF_A:OFF_B], a_sink, rope, need_ctx)
    y_b, z_b = _mixer_b(p[..., OFF_B:OFF_C], pc[..., OFF_B:OFF_C], b_mu, b_w0, b_wup, b_a0, b_aup,
                        b_kk, b_ka, b_rk, b_lng, b_lnb, need_ctx)
    y_c, z_c = _mixer_c(p[..., OFF_C:OFF_D], pc[..., OFF_C:OFF_D], c_qn, c_kn, rope, need_ctx)
    y_d, z_d = _mixer_d(p[..., OFF_D:OFF_G], pc[..., OFF_D:OFF_G], d_conv, d_alog, d_dtb, d_norm, need_ctx)
    x = x + gate[:, None] * _merge([y_a, y_b, y_c, y_d], p[..., OFF_G:], g_up, g_b, w_br, w_out)
    if need_ctx:
        xc = xc + gt_c * _merge([z_a, z_b, z_c, z_d], pc[..., OFF_G:], g_up, g_b, w_br, w_out)
    return x, xc


def setup_inputs(seed: int = 0) -> dict:
    key = jax.random.key(seed)
    ks = iter(jax.random.split(key, 32))
    L, D, W = DEPTH, D_MODEL, BRANCH_W

    def nrm(shape, s):
        return jax.random.normal(next(ks), shape, F32) * s

    def uni(shape, lo, hi):
        return jax.random.uniform(next(ks), shape, F32, lo, hi)

    x = nrm((BATCH, SEQ, D), 1.0)
    c = nrm((BATCH, D), 1.0)
    ctx = nrm((BATCH, CTX_LEN, D), 1.0)
    c_ctx = nrm((D,), 1.0)
    norm_g = 1.0 + nrm((L, D), 0.02)
    w_mod = nrm((L, D, 3 * D), 0.5 * D ** -0.5)
    b_mod = nrm((L, 3 * D), 0.02)
    w_in = nrm((L, D, PROJ_W), D ** -0.5)
    a_sink = nrm((L, A_HEADS), 0.5)
    b_mu = uni((L, 2, 3 * W + 2 * B_LORA), 0.0, 0.5)
    b_w0 = nrm((L, 2, W), 1.0)
    b_wup = nrm((L, 2, B_LORA, W), 0.1)
    b_a0 = nrm((L, 2, W), 0.5)
    b_aup = nrm((L, 2, B_LORA, W), 0.1)
    b_kk = 0.85 + nrm((L, W), 0.02)
    b_ka = 1.0 + nrm((L, W), 0.02)
    b_rk = nrm((L, B_HEADS, HEAD_DIM), 0.1)
    b_lng = 1.0 + nrm((L, W), 0.02)
    b_lnb = nrm((L, W), 0.02)
    c_qn = 1.0 + nrm((L, HEAD_DIM), 0.02)
    c_kn = 1.0 + nrm((L, HEAD_DIM), 0.02)
    d_conv = nrm((L, D_CONV, 3 * W), D_CONV ** -0.5)
    d_alog = jnp.log(uni((L, 2, D_HEADS), 1.0, 16.0))
    dt = jnp.exp(uni((L, 2, D_HEADS), math.log(1e-3), math.log(1e-1)))
    d_dtb = dt + jnp.log(-jnp.expm1(-dt))
    d_norm = 1.0 + nrm((L, D_HD), 0.02)
    g_up = nrm((L, N_BRANCH, GATE_RANK, D), GATE_RANK ** -0.5)
    g_b = nrm((L, N_BRANCH, D), 0.02)
    w_br = nrm((L, N_BRANCH, W, D), W ** -0.5)
    w_out = nrm((L, D, D), D ** -0.5)
    final_g = 1.0 + nrm((D,), 0.02)
    return {'x': x, 'c': c, 'ctx': ctx, 'c_ctx': c_ctx, 'norm_g': norm_g, 'w_mod': w_mod,
            'b_mod': b_mod, 'w_in': w_in, 'a_sink': a_sink, 'b_mu': b_mu, 'b_w0': b_w0,
            'b_wup': b_wup, 'b_a0': b_a0, 'b_aup': b_aup, 'b_kk': b_kk, 'b_ka': b_ka,
            'b_rk': b_rk, 'b_lng': b_lng, 'b_lnb': b_lnb, 'c_qn': c_qn, 'c_kn': c_kn,
            'd_conv': d_conv, 'd_alog': d_alog, 'd_dtb': d_dtb, 'd_norm': d_norm,
            'g_up': g_up, 'g_b': g_b, 'w_br': w_br, 'w_out': w_out, 'final_g': final_g}


def reference(x, c, ctx, c_ctx, norm_g, w_mod, b_mod, w_in, a_sink, b_mu, b_w0, b_wup, b_a0,
              b_aup, b_kk, b_ka, b_rk, b_lng, b_lnb, c_qn, c_kn, d_conv, d_alog, d_dtb, d_norm,
              g_up, g_b, w_br, w_out, final_g):
    n = x.shape[1]
    rows = n // GRID_W
    rope = _rope_tables(rows)
    xc = ctx
    for l in range(DEPTH):
        x, xc = _layer(x, xc, c, c_ctx, rope, l < DEPTH - 1, norm_g[l], w_mod[l], b_mod[l],
                       w_in[l], a_sink[l], b_mu[l], b_w0[l], b_wup[l], b_a0[l], b_aup[l],
                       b_kk[l], b_ka[l], b_rk[l], b_lng[l], b_lnb[l], c_qn[l], c_kn[l],
                       d_conv[l], d_alog[l], d_dtb[l], d_norm[l], g_up[l], g_b[l], w_br[l],
                       w_out[l])
    return _rms(x, final_g)
```

```python
import functools
import math

import jax
import jax.numpy as jnp
from jax import lax
from jax.experimental import pallas as pl
from jax.experimental.pallas import tpu as pltpu

F32 = jnp.float32
BF16 = jnp.bfloat16
HI = lax.Precision.HIGHEST

D_MODEL = 2048
DEPTH = 4
GRID_W = 64
CTX = 256
N_BRANCH = 4
BW = D_MODEL // 4
HD = 64
N_HEADS = BW // HD
N_KV = 2
KVW = N_KV * HD
WINDOW = 128
B_LORA = 64
B_DECAY_SCALE = 0.606531
B_GN_EPS = 64e-5
D_HD = 128
D_HEADS = BW // D_HD
D_CONV = 5
GATE_RANK = D_MODEL // 8
ROPE_THETA = 10000.0
EPS = 1e-6
NEG = -1e30
CHUNK = 64

P_AQ, P_AG, P_CQ, P_CG, P_BG, P_DG = 0, 512, 1024, 1536, 2048, 2560
P_BRKV, P_DQKV = 3072, 4608
P_AKV, P_CKV, P_G = 6144, 6400, 6656
P_BL, P_DAB = 6912, 7040
P_W = 7168

VMEM_LIMIT = 56 * 1024 * 1024


def _cparams(sem):
    return pltpu.CompilerParams(dimension_semantics=sem, vmem_limit_bytes=VMEM_LIMIT)


def _dot(a, b, prec=None):
    return jnp.dot(a, b, precision=prec, preferred_element_type=F32)


def _dot_nt(a, b, prec=None):
    return lax.dot_general(a, b, (((1,), (1,)), ((), ())), precision=prec,
                           preferred_element_type=F32)


def _dot_tn(a, b, prec=None):
    return lax.dot_general(a, b, (((0,), (0,)), ((), ())), precision=prec,
                           preferred_element_type=F32)


def _mod_kernel(c_ref, w_ref, b_ref, o_ref):
    c = c_ref[...]
    s = c * jax.nn.sigmoid(c)
    o_ref[0] = _dot(s, w_ref[0], HI) + b_ref[0]


def _mod_call(cs, w_mod, b_mod):
    L, D, D3 = w_mod.shape
    tn = 768
    return pl.pallas_call(
        _mod_kernel,
        grid=(L, D3 // tn),
        in_specs=[pl.BlockSpec((8, D), lambda l, j: (0, 0)),
                  pl.BlockSpec((1, D, tn), lambda l, j: (l, 0, j)),
                  pl.BlockSpec((1, 1, tn), lambda l, j: (l, 0, j))],
        out_specs=pl.BlockSpec((1, 8, tn), lambda l, j: (l, 0, j)),
        out_shape=jax.ShapeDtypeStruct((L, 8, D3), F32),
        compiler_params=_cparams(("parallel", "parallel")),
        name="mod",
    )(cs, w_mod, b_mod.reshape(L, 1, D3))


def _inproj_kernel(x_ref, mod_ref, g_ref, w_ref, o_ref, h_ref, *, tm):
    @pl.when(pl.program_id(2) == 0)
    def _():
        x = x_ref[0]
        ms = jnp.mean(x * x, axis=-1, keepdims=True)
        y = x * lax.rsqrt(ms + EPS) * g_ref[...]
        row = pl.program_id(1) * tm + lax.broadcasted_iota(jnp.int32, (tm, 1), 0)
        is_ctx = row < CTX
        m = mod_ref[0]
        scale = jnp.where(is_ctx, m[2:3], m[0:1])
        shift = jnp.where(is_ctx, m[3:4], m[1:2])
        h_ref[...] = (y * (1.0 + scale) + shift).astype(BF16)

    o_ref[0] = _dot(h_ref[...], w_ref[...])


def _inproj_call(x, modv, norm_g, w_pad, tm=768, tn=512):
    B, T, D = x.shape
    return pl.pallas_call(
        functools.partial(_inproj_kernel, tm=tm),
        grid=(B, T // tm, P_W // tn),
        in_specs=[pl.BlockSpec((1, tm, D), lambda b, i, j: (b, i, 0)),
                  pl.BlockSpec((1, 8, D), lambda b, i, j: (b, 0, 0)),
                  pl.BlockSpec((1, D), lambda b, i, j: (0, 0)),
                  pl.BlockSpec((D, tn), lambda b, i, j: (0, j))],
        out_specs=pl.BlockSpec((1, tm, tn), lambda b, i, j: (b, i, j)),
        out_shape=jax.ShapeDtypeStruct((B, T, P_W), F32),
        scratch_shapes=[pltpu.VMEM((tm, D), BF16)],
        compiler_params=_cparams(("parallel", "parallel", "arbitrary")),
        name="inproj",
    )(x, modv, norm_g.reshape(1, D), w_pad)


def _rope128(x, cos, sin_signed):
    lane = lax.broadcasted_iota(jnp.int32, x.shape, 1)
    lo = (lane % 32) < 16
    rot = jnp.where(lo, pltpu.roll(x, 128 - 16, 1), pltpu.roll(x, 16, 1))
    return x * cos + rot * sin_signed


def _head_ms(x):
    r = lax.broadcasted_iota(jnp.int32, (128, 128), 0) // HD
    c = lax.broadcasted_iota(jnp.int32, (128, 128), 1) // HD
    ones = jnp.where(r == c, 1.0 / HD, 0.0).astype(F32)
    return _dot(x * x, ones, HI)


def _attn_prep_kernel(q_ref, kv_ref, cos_ref, sin_ref, qn_ref, kn_ref, qz_ref, k_ref, v_ref,
                      *, use_norm):
    cos = cos_ref[...]
    sin = sin_ref[...]
    tm = cos.shape[0]
    lane = lax.broadcasted_iota(jnp.int32, (tm, 128), 1)
    scale = HD ** -0.5
    kv = kv_ref[0]
    k = kv[:, :KVW]
    if use_norm:
        k = k * lax.rsqrt(_head_ms(k) + EPS) * kn_ref[...]
    k_ref[0] = _rope128(k, cos, sin).astype(BF16)
    v_ref[0] = kv[:, KVW:].astype(BF16)
    q = q_ref[0]
    for p in range(N_HEADS // 2):
        qp = q[:, 128 * p:128 * (p + 1)]
        if use_norm:
            qp = qp * lax.rsqrt(_head_ms(qp) + EPS) * qn_ref[...]
        qp = _rope128(qp, cos, sin) * scale
        g = (2 * p) // (N_HEADS // N_KV)
        other = pltpu.roll(qp, 64, 1)
        if g == 0:
            h0 = jnp.where(lane < 64, qp, 0.0)
            h1 = jnp.where(lane < 64, other, 0.0)
        else:
            h0 = jnp.where(lane >= 64, other, 0.0)
            h1 = jnp.where(lane >= 64, qp, 0.0)
        qz_ref[0, :, 256 * p:256 * p + 128] = h0.astype(BF16)
        qz_ref[0, :, 256 * p + 128:256 * p + 256] = h1.astype(BF16)


def _attn_prep_call(p, cos, sin, qn, kn, q_off, kv_off, use_norm, tm=256):
    B, T, _ = p.shape
    return pl.pallas_call(
        functools.partial(_attn_prep_kernel, use_norm=use_norm),
        grid=(B, T // tm),
        in_specs=[pl.BlockSpec((1, tm, BW), lambda b, i: (b, i, q_off // BW)),
                  pl.BlockSpec((1, tm, 2 * KVW), lambda b, i: (b, i, kv_off // (2 * KVW))),
                  pl.BlockSpec((tm, 128), lambda b, i: (i, 0)),
                  pl.BlockSpec((tm, 128), lambda b, i: (i, 0)),
                  pl.BlockSpec((1, 128), lambda b, i: (0, 0)),
                  pl.BlockSpec((1, 128), lambda b, i: (0, 0))],
        out_specs=[pl.BlockSpec((1, tm, N_HEADS * 128), lambda b, i: (b, i, 0)),
                   pl.BlockSpec((1, tm, KVW), lambda b, i: (b, i, 0)),
                   pl.BlockSpec((1, tm, KVW), lambda b, i: (b, i, 0))],
        out_shape=[jax.ShapeDtypeStruct((B, T, N_HEADS * 128), BF16),
                   jax.ShapeDtypeStruct((B, T, KVW), BF16),
                   jax.ShapeDtypeStruct((B, T, KVW), BF16)],
        compiler_params=_cparams(("parallel", "parallel")),
        name="attn_prep",
    )(p, p, cos, sin, qn, kn)


def _flash_kernel(sink_ref, qz_ref, k_ref, v_ref, g_ref, o_ref, m_ref, l_ref, acc_ref,
                  *, window, tq, tk, n_kv_steps):
    qi = pl.program_id(1)
    ks = pl.program_id(2)

    @pl.when(ks == 0)
    def _():
        for h in range(N_HEADS):
            if window:
                m_ref[h] = jnp.full((tq, 128), sink_ref[h], F32)
                l_ref[h] = jnp.ones((tq, 128), F32)
            else:
                m_ref[h] = jnp.full((tq, 128), NEG, F32)
                l_ref[h] = jnp.zeros((tq, 128), F32)
        acc_ref[...] = jnp.zeros_like(acc_ref)

    def step(mask):
        k = k_ref[0]
        v = v_ref[0]
        for h in range(N_HEADS):
            s = _dot_nt(qz_ref[0, :, 128 * h:128 * (h + 1)], k)
            if mask is not None:
                s = jnp.where(mask, s, NEG)
            m_prev = m_ref[h]
            m_new = jnp.maximum(m_prev, jnp.max(s, axis=-1, keepdims=True))
            alpha = jnp.exp(m_prev - m_new)
            pr = jnp.exp(s - m_new[:, :1])
            if mask is not None:
                pr = jnp.where(mask, pr, 0.0)
            l_ref[h] = alpha * l_ref[h] + jnp.sum(pr, axis=-1, keepdims=True)
            m_ref[h] = m_new
            acc_ref[h] = alpha * acc_ref[h] + _dot(pr.astype(BF16), v)

    row = lax.broadcasted_iota(jnp.int32, (tq, tk), 0)
    col = lax.broadcasted_iota(jnp.int32, (tq, tk), 1)
    if window:
        n_tiles = pl.num_programs(1)

        @pl.when(ks == 0)
        def _():
            step(None)

        kt = qi + ks - 2

        @pl.when((ks > 0) & (qi > 0) & (kt >= 1) & (kt < n_tiles))
        def _():
            d = row - col + (2 - ks) * tq
            step(jnp.abs(d) <= WINDOW)
    else:
        q_has_ctx = qi * tq < CTX

        @pl.when(jnp.logical_not(q_has_ctx))
        def _():
            step(None)

        @pl.when(q_has_ctx)
        def _():
            step((row + qi * tq >= CTX) | (col + ks * tk < CTX))

    @pl.when(ks == n_kv_steps - 1)
    def _():
        lane = lax.broadcasted_iota(jnp.int32, (tq, 128), 1)
        for p in range(N_HEADS // 2):
            g = (2 * p) // (N_HEADS // N_KV)
            a0 = acc_ref[2 * p] / l_ref[2 * p]
            a1 = acc_ref[2 * p + 1] / l_ref[2 * p + 1]
            if g == 0:
                o = jnp.where(lane < 64, a0, pltpu.roll(a1, 64, 1))
            else:
                o = jnp.where(lane < 64, pltpu.roll(a0, 64, 1), a1)
            gate = g_ref[0, :, 128 * p:128 * (p + 1)]
            o_ref[0, :, 128 * p:128 * (p + 1)] = o * (gate * jax.nn.sigmoid(gate))


def _flash_call(sink, qz, k, v, p, g_off, window, tq=256, tk=768):
    B, T, _ = qz.shape
    if window:
        tq = tk = CTX
        n_kv_steps = 4
        n_tiles = T // tq

        def kv_map(b, i, s):
            return (b, jnp.where(s == 0, 0, jnp.clip(i + s - 2, 0, n_tiles - 1)), 0)
    else:
        n_kv_steps = T // tk

        def kv_map(b, i, s):
            return (b, s, 0)

    return pl.pallas_call(
        functools.partial(_flash_kernel, window=window, tq=tq, tk=tk, n_kv_steps=n_kv_steps),
        grid=(B, T // tq, n_kv_steps),
        in_specs=[pl.BlockSpec(memory_space=pltpu.SMEM),
                  pl.BlockSpec((1, tq, N_HEADS * 128), lambda b, i, s: (b, i, 0)),
                  pl.BlockSpec((1, tk, KVW), kv_map),
                  pl.BlockSpec((1, tk, KVW), kv_map),
                  pl.BlockSpec((1, tq, BW), lambda b, i, s: (b, i, g_off // BW))],
        out_specs=pl.BlockSpec((1, tq, BW), lambda b, i, s: (b, i, 0)),
        out_shape=jax.ShapeDtypeStruct((B, T, BW), F32),
        scratch_shapes=[pltpu.VMEM((N_HEADS, tq, 128), F32),
                        pltpu.VMEM((N_HEADS, tq, 128), F32),
                        pltpu.VMEM((N_HEADS, tq, 128), F32)],
        compiler_params=_cparams(("parallel", "parallel", "arbitrary")),
        name="flash_window" if window else "flash_global",
    )(sink, qz, k, v, p)


def _chunk_masks(d):
    row = lax.broadcasted_iota(jnp.int32, (CHUNK, CHUNK), 0)
    col = lax.broadcasted_iota(jnp.int32, (CHUNK, CHUNK), 1)
    lag = (row - col) * (1 - 2 * d)
    return row, col, lag > 0, lag >= 0, lag <= 0


def _unit_tri_inverse(a, row, col):
    eye = (row == col).astype(F32)
    t = eye
    s = 1
    while s < CHUNK:
        off = ((row // (2 * s)) == (col // (2 * s))) & ((row // s) != (col // s))
        am = jnp.where(off, a, 0.0)
        if s == 1:
            t = eye - am
        else:
            t = t - _dot(t, _dot(am, t, HI), HI)
        s *= 2
    return t


def _scan_chunk_index(p, d, n_ctx_chunks, n_chunks):
    rev_idx = jnp.where(p < n_ctx_chunks, n_ctx_chunks - 1 - p, n_chunks - 1 + n_ctx_chunks - p)
    return jnp.where(d == 0, p, rev_idx)


def _rwkv_kernel(r_ref, v_ref, kk_ref, lw_ref, a_ref, kt_ref, y_ref, s_ref):
    d = pl.program_id(1)

    @pl.when(pl.program_id(2) == 0)
    def _():
        s_ref[...] = jnp.zeros_like(s_ref)

    row, col, strict, incl, _ = _chunk_masks(d)
    lw = lw_ref[0, 0]
    g = _dot(incl.astype(F32), lw, HI)
    g_last = jnp.sum(lw, axis=0, keepdims=True)
    eg = jnp.exp(g)
    egx = jnp.exp(g - lw)
    ieg = jnp.exp(-g)
    egl = jnp.exp(g_last)
    kk = kk_ref[0]
    alpha = kk * a_ref[0, 0]
    kp_all = kk * egx
    rg_all = r_ref[0] * eg
    kd_all = kt_ref[0, 0] * ieg
    ad_all = alpha * ieg
    v_all = v_ref[0]
    for h in range(N_HEADS):
        sl = slice(HD * h, HD * (h + 1))
        kp, rg, kd, ad, v = kp_all[:, sl], rg_all[:, sl], kd_all[:, sl], ad_all[:, sl], v_all[:, sl]
        a_vk = jnp.where(strict, _dot_nt(kp, kd, HI), 0.0)
        a_ua = jnp.where(strict, _dot_nt(kp, ad, HI), 0.0)
        b_rk = jnp.where(incl, _dot_nt(rg, kd, HI), 0.0)
        b_ra = jnp.where(incl, _dot_nt(rg, ad, HI), 0.0)
        t = _unit_tri_inverse(a_ua, row, col)
        s0 = s_ref[h]
        rhs = _dot_nt(kp, s0, HI) + _dot(a_vk, v, HI)
        u = _dot(t, rhs, HI)
        y = _dot_nt(rg, s0, HI) + _dot(b_rk, v, HI) - _dot(b_ra, u, HI)
        y_ref[0, 0, :, sl] = y
        el = egl[:, sl]
        s_ref[h] = s0 * el + _dot_tn(v, kd * el, HI) - _dot_tn(u, ad * el, HI)


def _rwkv_call(r, v, kk, lw, a, kt):
    B, T, W = r.shape
    nc = T // CHUNK
    ncc = CTX // CHUNK

    def tok(b, d, p):
        return (b, _scan_chunk_index(p, d, ncc, nc), 0)

    def tokd(b, d, p):
        return (d, b, _scan_chunk_index(p, d, ncc, nc), 0)

    return pl.pallas_call(
        _rwkv_kernel,
        grid=(B, 2, nc),
        in_specs=[pl.BlockSpec((1, CHUNK, W), tok)] * 3 + [pl.BlockSpec((1, 1, CHUNK, W), tokd)] * 3,
        out_specs=pl.BlockSpec((1, 1, CHUNK, W), tokd),
        out_shape=jax.ShapeDtypeStruct((2, B, T, W), F32),
        scratch_shapes=[pltpu.VMEM((N_HEADS, HD, HD), F32)],
        compiler_params=_cparams(("parallel", "parallel", "arbitrary")),
        name="rwkv_scan",
    )(r, v, kk, lw, a, kt)


def _gdn_kernel(q_ref, k_ref, v_ref, lg_ref, beta_ref, o_ref, s_ref):
    d = pl.program_id(1)

    @pl.when(pl.program_id(2) == 0)
    def _():
        s_ref[...] = jnp.zeros_like(s_ref)

    row, col, strict, incl, incl_t = _chunk_masks(d)
    lg_all = lg_ref[0, 0]
    beta_all = beta_ref[0, 0]
    gc_all = _dot(incl.astype(F32), lg_all, HI)
    gct_all = _dot_tn(lg_all, incl_t.astype(F32), HI)
    gl_all = jnp.sum(lg_all, axis=0, keepdims=True)
    for h in range(D_HEADS):
        sl = slice(D_HD * h, D_HD * (h + 1))
        q, k, v = q_ref[0, :, sl], k_ref[0, :, sl], v_ref[0, :, sl]
        gc = gc_all[:, h:h + 1]
        gct = gct_all[h:h + 1, :]
        beta = beta_all[:, h:h + 1]
        gl = gl_all[:, h:h + 1]
        dmask = jnp.where(incl, jnp.exp(jnp.where(incl, gc - gct, 0.0)), 0.0)
        kb = k * beta
        a_mat = jnp.where(strict, _dot_nt(kb, k, HI) * dmask, 0.0)
        t = _unit_tri_inverse(a_mat, row, col)
        eg = jnp.exp(gc)
        u = _dot(t, v * beta, HI)
        wk = _dot(t, kb * eg, HI)
        qk = _dot_nt(q, k, HI) * dmask
        s0 = s_ref[h]
        vnew = u - _dot(wk, s0, HI)
        o_ref[0, 0, :, sl] = _dot(q * eg, s0, HI) + _dot(qk, vnew, HI)
        kd = k * jnp.exp(gl - gc)
        s_ref[h] = s0 * jnp.exp(gl) + _dot_tn(kd, vnew, HI)


def _gdn_call(q, k, v, lg, beta):
    B, T, W = q.shape
    nc = T // CHUNK
    ncc = CTX // CHUNK

    def tok(b, d, p):
        return (b, _scan_chunk_index(p, d, ncc, nc), 0)

    def tokd(b, d, p):
        return (d, b, _scan_chunk_index(p, d, ncc, nc), 0)

    return pl.pallas_call(
        _gdn_kernel,
        grid=(B, 2, nc),
        in_specs=[pl.BlockSpec((1, CHUNK, W), tok)] * 3 + [pl.BlockSpec((1, 1, CHUNK, 128), tokd)] * 2,
        out_specs=pl.BlockSpec((1, 1, CHUNK, W), tokd),
        out_shape=jax.ShapeDtypeStruct((2, B, T, W), F32),
        scratch_shapes=[pltpu.VMEM((D_HEADS, D_HD, D_HD), F32)],
        compiler_params=_cparams(("parallel", "parallel", "arbitrary")),
        name="gdn_scan",
    )(q, k, v, lg, beta)


def _branch_kernel(pm_ref, ya_ref, yb_ref, yc_ref, yd_ref, gup_ref, gb_ref, wbr_ref, o_ref):
    pm = pm_ref[0].astype(BF16)
    acc = None
    for i, y_ref in enumerate((ya_ref, yb_ref, yc_ref, yd_ref)):
        gate = jax.nn.sigmoid(_dot(pm, gup_ref[i]) + gb_ref[i])
        term = gate * _dot(y_ref[0].astype(BF16), wbr_ref[i])
        acc = term if acc is None else acc + term
    o_ref[0] = acc.astype(BF16)


def _branch_call(p, ys, g_up, g_b, w_br, tm=256):
    B, T, _ = p.shape
    D = D_MODEL
    tok = lambda b, i: (b, i, 0)
    return pl.pallas_call(
        _branch_kernel,
        grid=(B, T // tm),
        in_specs=[pl.BlockSpec((1, tm, GATE_RANK), lambda b, i: (b, i, P_G // GATE_RANK))]
        + [pl.BlockSpec((1, tm, BW), tok)] * 4
        + [pl.BlockSpec((N_BRANCH, GATE_RANK, D), lambda b, i: (0, 0, 0)),
           pl.BlockSpec((N_BRANCH, 1, D), lambda b, i: (0, 0, 0)),
           pl.BlockSpec((N_BRANCH, BW, D), lambda b, i: (0, 0, 0))],
        out_specs=pl.BlockSpec((1, tm, D), tok),
        out_shape=jax.ShapeDtypeStruct((B, T, D), BF16),
        compiler_params=_cparams(("parallel", "parallel")),
        name="branch_merge",
    )(p, *ys, g_up, g_b.reshape(N_BRANCH, 1, D), w_br)


def _outproj_kernel(x_ref, acc_ref, w_ref, mod_ref, fg_ref, o_ref, *, tm, tile_off, final):
    row = (pl.program_id(1) + tile_off) * tm + lax.broadcasted_iota(jnp.int32, (tm, 1), 0)
    m = mod_ref[0]
    gate = jnp.where(row < CTX, m[5:6], m[4:5])
    y = x_ref[0] + gate * _dot(acc_ref[0], w_ref[...])
    if final:
        ms = jnp.mean(y * y, axis=-1, keepdims=True)
        y = y * lax.rsqrt(ms + EPS) * fg_ref[...]
    o_ref[0] = y


def _outproj_call(x, acc, w_out, modv, final_g, final, tm=256):
    B, T, D = x.shape
    tile_off = CTX // tm if final else 0
    t_out = T - CTX if final else T
    tok_in = lambda b, i: (b, i + tile_off, 0)
    return pl.pallas_call(
        functools.partial(_outproj_kernel, tm=tm, tile_off=tile_off, final=final),
        grid=(B, t_out // tm),
        in_specs=[pl.BlockSpec((1, tm, D), tok_in),
                  pl.BlockSpec((1, tm, D), tok_in),
                  pl.BlockSpec((D, D), lambda b, i: (0, 0)),
                  pl.BlockSpec((1, 8, D), lambda b, i: (b, 0, 0)),
                  pl.BlockSpec((1, D), lambda b, i: (0, 0))],
        out_specs=pl.BlockSpec((1, tm, D), lambda b, i: (b, i, 0)),
        out_shape=jax.ShapeDtypeStruct((B, t_out, D), F32),
        compiler_params=_cparams(("parallel", "parallel")),
        name="outproj",
    )(x, acc, w_out, modv, final_g.reshape(1, D))


def _pad_w_in(w):
    oa, ob, oc, od = 0, 1280, 3456, 4736
    og = od + 2064
    D = w.shape[0]
    z = lambda n: jnp.zeros((D, n), w.dtype)
    a_q, a_kv, a_g = w[:, oa:oa + 512], w[:, oa + 512:oa + 768], w[:, oa + 768:oa + 1280]
    b_rkv, b_l, b_g = w[:, ob:ob + 1536], w[:, ob + 1536:ob + 1664], w[:, ob + 1664:ob + 2176]
    c_q, c_kv, c_g = w[:, oc:oc + 512], w[:, oc + 512:oc + 768], w[:, oc + 768:oc + 1280]
    d_qkv, d_ab, d_g = w[:, od:od + 1536], w[:, od + 1536:od + 1552], w[:, od + 1552:od + 2064]
    g = w[:, og:og + GATE_RANK]
    cols = [a_q, a_g, c_q, c_g, b_g, d_g, b_rkv, d_qkv, a_kv, c_kv, g, b_l, d_ab, z(112)]
    return jnp.concatenate(cols, axis=1).astype(BF16)


def _rope_tables(n):
    rows = n // GRID_W
    row = jnp.repeat(jnp.arange(rows, dtype=F32), GRID_W)
    col = jnp.tile(jnp.arange(GRID_W, dtype=F32), rows)
    half = HD // 2
    inv = ROPE_THETA ** (-jnp.arange(0, half, 2, dtype=F32) / half)
    ar = row[:, None] * inv
    ac = col[:, None] * inv
    cos = jnp.concatenate([jnp.cos(ar), jnp.cos(ar), jnp.cos(ac), jnp.cos(ac)], axis=-1)
    sin = jnp.concatenate([-jnp.sin(ar), jnp.sin(ar), -jnp.sin(ac), jnp.sin(ac)], axis=-1)
    cos = jnp.concatenate([jnp.ones((CTX, HD), F32), cos], axis=0)
    sin = jnp.concatenate([jnp.zeros((CTX, HD), F32), sin], axis=0)
    return jnp.tile(cos, (1, 2)), jnp.tile(sin, (1, 2))


def _seg_shift(z, k):
    def sh(a):
        if k > 0:
            return jnp.pad(a[:, :-k], ((0, 0), (k, 0), (0, 0)))
        return jnp.pad(a[:, -k:], ((0, 0), (0, -k), (0, 0)))
    return jnp.concatenate([sh(z[:, :CTX]), sh(z[:, CTX:])], axis=1)


def _group_sum(x, hd):
    s = x.reshape(x.shape[:-1] + (x.shape[-1] // hd, hd)).sum(-1, keepdims=True)
    return jnp.broadcast_to(s, s.shape[:-1] + (hd,)).reshape(x.shape)


def _silu(x):
    return x * jax.nn.sigmoid(x)


def _mixer_b(p, mu, w0, wup, a0, aup, k_k, k_a, r_k, ln_g, ln_b):
    W = BW
    z = jnp.concatenate([p[..., P_BRKV:P_BRKV + 3 * W], p[..., P_BL:P_BL + 2 * B_LORA]], axis=-1)
    g = p[..., P_BG:P_BG + W]
    z = z + mu[0] * (_seg_shift(z, 1) - z) + mu[1] * (_seg_shift(z, -1) - z)
    r, k, v = z[..., :W], z[..., W:2 * W], z[..., 2 * W:3 * W]
    wl = jnp.tanh(z[..., 3 * W:3 * W + B_LORA])
    al = z[..., 3 * W + B_LORA:]
    kk = k * k_k
    kk = kk * lax.rsqrt(jnp.maximum(_group_sum(kk * kk, HD), EPS * EPS))
    lws, as_, kts, bonus = [], [], [], 0.0
    rk = r_k.reshape(W)
    for d in range(2):
        lw = -B_DECAY_SCALE * jax.nn.sigmoid(w0[d] + jnp.dot(wl, wup[d], precision=HI))
        a = jax.nn.sigmoid(a0[d] + jnp.dot(al, aup[d], precision=HI))
        kt = k * (1.0 + (a - 1.0) * k_a)
        lws.append(lw)
        as_.append(a)
        kts.append(kt)
        bonus = bonus + _group_sum(r * kt * rk, HD) * v
    y = _rwkv_call(r, v, kk, jnp.stack(lws), jnp.stack(as_), jnp.stack(kts))
    wkv = y[0] + y[1]
    mean = _group_sum(wkv, HD) / HD
    cen = wkv - mean
    var = _group_sum(cen * cen, HD) / HD
    gn = cen * lax.rsqrt(var + B_GN_EPS) * ln_g + ln_b
    return (gn + bonus) * _silu(g)


def _mixer_d(p, conv_w, alog, dtb, norm_g):
    W, H = BW, D_HEADS
    zin = p[..., P_DQKV:P_DQKV + 3 * W]
    g = p[..., P_DG:P_DG + W]
    conv = sum(conv_w[i] * _seg_shift(zin, D_CONV // 2 - i) if i != D_CONV // 2 else conv_w[i] * zin
               for i in range(D_CONV))
    qkv = _silu(conv)
    q, k, v = qkv[..., :W], qkv[..., W:2 * W], qkv[..., 2 * W:]
    q = q * lax.rsqrt(jnp.maximum(_group_sum(q * q, D_HD), EPS * EPS)) * (D_HD ** -0.5)
    k = k * lax.rsqrt(jnp.maximum(_group_sum(k * k, D_HD), EPS * EPS))
    ab = p[..., P_DAB:P_DAB + 4 * H]
    lgs, betas = [], []
    for d in range(2):
        a_raw = ab[..., 2 * d * H:(2 * d + 1) * H]
        b_raw = ab[..., (2 * d + 1) * H:(2 * d + 2) * H]
        lg = -jnp.exp(alog[d]) * jax.nn.softplus(a_raw + dtb[d])
        pad = ((0, 0), (0, 0), (0, 128 - H))
        lgs.append(jnp.pad(lg, pad))
        betas.append(jnp.pad(jax.nn.sigmoid(b_raw), pad))
    o = _gdn_call(q, k, v, jnp.stack(lgs), jnp.stack(betas))
    o = o[0] + o[1]
    ms = _group_sum(o * o, D_HD) / D_HD
    return o * lax.rsqrt(ms + EPS) * jnp.tile(norm_g, H) * _silu(g)


def kernel(x, c, ctx, c_ctx, norm_g, w_mod, b_mod, w_in, a_sink, b_mu, b_w0, b_wup, b_a0, b_aup,
           b_kk, b_ka, b_rk, b_lng, b_lnb, c_qn, c_kn, d_conv, d_alog, d_dtb, d_norm, g_up, g_b,
           w_br, w_out, final_g):
    B, n, D = x.shape
    L = w_mod.shape[0]
    T = CTX + n
    xa = jnp.concatenate([ctx, x], axis=1)
    cs = jnp.zeros((8, D), F32).at[:B].set(c).at[B].set(c_ctx)
    mod = _mod_call(cs, w_mod, b_mod)
    cos, sin = _rope_tables(n)
    ones = jnp.ones((1, 128), F32)
    tm_in = 768 if T % 768 == 0 else 256
    for l in range(L):
        shift, scale, gate = jnp.split(mod[l], 3, axis=-1)
        bc = lambda v: jnp.broadcast_to(v[B], (B, D))
        modv = jnp.stack([scale[:B], shift[:B], bc(scale), bc(shift), gate[:B], bc(gate),
                          jnp.zeros((B, D), F32), jnp.zeros((B, D), F32)], axis=1)
        p = _inproj_call(xa, modv, norm_g[l], _pad_w_in(w_in[l]), tm=tm_in)
        qz, k, v = _attn_prep_call(p, cos, sin, ones, ones, P_AQ, P_AKV, False)
        y_a = _flash_call(a_sink[l], qz, k, v, p, P_AG, True)
        qz, k, v = _attn_prep_call(p, cos, sin, jnp.tile(c_qn[l], 2)[None], jnp.tile(c_kn[l], 2)[None],
                                   P_CQ, P_CKV, True)
        y_c = _flash_call(a_sink[l], qz, k, v, p, P_CG, False, tk=tm_in)
        y_b = _mixer_b(p, b_mu[l], b_w0[l], b_wup[l], b_a0[l], b_aup[l], b_kk[l], b_ka[l], b_rk[l],
                       b_lng[l], b_lnb[l])
        y_d = _mixer_d(p, d_conv[l], d_alog[l], d_dtb[l], d_norm[l])
        acc = _branch_call(p, (y_a, y_b, y_c, y_d), g_up[l].astype(BF16), g_b[l], w_br[l].astype(BF16))
        xa = _outproj_call(xa, acc, w_out[l].astype(BF16), modv, final_g, l == L - 1)
    return xa
```

```python
import functools
import math

import jax
import jax.numpy as jnp
from jax import lax
from jax.experimental import pallas as pl
from jax.experimental.pallas import tpu as pltpu

F32 = jnp.float32
BF16 = jnp.bfloat16
HI = lax.Precision.HIGHEST

D_MODEL = 2048
DEPTH = 4
GRID_W = 64
CTX = 256
N_BRANCH = 4
BW = D_MODEL // 4
HD = 64
N_HEADS = BW // HD
N_KV = 2
KVW = N_KV * HD
WINDOW = 128
B_LORA = 64
B_DECAY_SCALE = 0.606531
B_GN_EPS = 64e-5
D_HD = 128
D_HEADS = BW // D_HD
D_CONV = 5
GATE_RANK = D_MODEL // 8
ROPE_THETA = 10000.0
EPS = 1e-6
NEG = -1e30
CHUNK = 64

P_AQ, P_AG, P_CQ, P_CG, P_BG, P_DG = 0, 512, 1024, 1536, 2048, 2560
P_BRKV, P_DQKV = 3072, 4608
P_AKV, P_CKV, P_G = 6144, 6400, 6656
P_BL, P_DAB = 6912, 7040
P_W = 7168

VMEM_LIMIT = 56 * 1024 * 1024


def _cparams(sem):
    return pltpu.CompilerParams(dimension_semantics=sem, vmem_limit_bytes=VMEM_LIMIT)


def _dot(a, b, prec=None):
    return jnp.dot(a, b, precision=prec, preferred_element_type=F32)


def _dot_nt(a, b, prec=None):
    return lax.dot_general(a, b, (((1,), (1,)), ((), ())), precision=prec,
                           preferred_element_type=F32)


def _dot_tn(a, b, prec=None):
    return lax.dot_general(a, b, (((0,), (0,)), ((), ())), precision=prec,
                           preferred_element_type=F32)


def _mod_kernel(c_ref, w_ref, b_ref, o_ref):
    c = c_ref[...]
    s = c * jax.nn.sigmoid(c)
    o_ref[0] = _dot(s, w_ref[0], HI) + b_ref[0]


def _mod_call(cs, w_mod, b_mod):
    L, D, D3 = w_mod.shape
    tn = 768
    return pl.pallas_call(
        _mod_kernel,
        grid=(L, D3 // tn),
        in_specs=[pl.BlockSpec((8, D), lambda l, j: (0, 0)),
                  pl.BlockSpec((1, D, tn), lambda l, j: (l, 0, j)),
                  pl.BlockSpec((1, 1, tn), lambda l, j: (l, 0, j))],
        out_specs=pl.BlockSpec((1, 8, tn), lambda l, j: (l, 0, j)),
        out_shape=jax.ShapeDtypeStruct((L, 8, D3), F32),
        compiler_params=_cparams(("parallel", "parallel")),
        name="mod",
    )(cs, w_mod, b_mod.reshape(L, 1, D3))


def _inproj_kernel(x_ref, mod_ref, g_ref, w_ref, o_ref, h_ref, *, tm):
    @pl.when(pl.program_id(2) == 0)
    def _():
        x = x_ref[0]
        ms = jnp.mean(x * x, axis=-1, keepdims=True)
        y = x * lax.rsqrt(ms + EPS) * g_ref[...]
        row = pl.program_id(1) * tm + lax.broadcasted_iota(jnp.int32, (tm, 1), 0)
        is_ctx = row < CTX
        m = mod_ref[0]
        scale = jnp.where(is_ctx, m[2:3], m[0:1])
        shift = jnp.where(is_ctx, m[3:4], m[1:2])
        h_ref[...] = (y * (1.0 + scale) + shift).astype(BF16)

    o_ref[0] = _dot(h_ref[...], w_ref[...])


def _inproj_call(x, modv, norm_g, w_pad, tm=768, tn=512):
    B, T, D = x.shape
    return pl.pallas_call(
        functools.partial(_inproj_kernel, tm=tm),
        grid=(B, T // tm, P_W // tn),
        in_specs=[pl.BlockSpec((1, tm, D), lambda b, i, j: (b, i, 0)),
                  pl.BlockSpec((1, 8, D), lambda b, i, j: (b, 0, 0)),
                  pl.BlockSpec((1, D), lambda b, i, j: (0, 0)),
                  pl.BlockSpec((D, tn), lambda b, i, j: (0, j))],
        out_specs=pl.BlockSpec((1, tm, tn), lambda b, i, j: (b, i, j)),
        out_shape=jax.ShapeDtypeStruct((B, T, P_W), F32),
        scratch_shapes=[pltpu.VMEM((tm, D), BF16)],
        compiler_params=_cparams(("parallel", "parallel", "arbitrary")),
        name="inproj",
    )(x, modv, norm_g.reshape(1, D), w_pad)


def _rope128(x, cos, sin_signed):
    lane = lax.broadcasted_iota(jnp.int32, x.shape, 1)
    lo = (lane % 32) < 16
    rot = jnp.where(lo, pltpu.roll(x, 128 - 16, 1), pltpu.roll(x, 16, 1))
    return x * cos + rot * sin_signed


def _head_ms(x):
    r = lax.broadcasted_iota(jnp.int32, (128, 128), 0) // HD
    c = lax.broadcasted_iota(jnp.int32, (128, 128), 1) // HD
    ones = jnp.where(r == c, 1.0 / HD, 0.0).astype(F32)
    return _dot(x * x, ones, HI)


def _attn_prep_kernel(q_ref, kv_ref, cos_ref, sin_ref, qn_ref, kn_ref, qz_ref, k_ref, v_ref,
                      *, use_norm):
    cos = cos_ref[...]
    sin = sin_ref[...]
    tm = cos.shape[0]
    lane = lax.broadcasted_iota(jnp.int32, (tm, 128), 1)
    scale = HD ** -0.5
    kv = kv_ref[0]
    k = kv[:, :KVW]
    if use_norm:
        k = k * lax.rsqrt(_head_ms(k) + EPS) * kn_ref[...]
    k_ref[0] = _rope128(k, cos, sin).astype(BF16)
    v_ref[0] = kv[:, KVW:].astype(BF16)
    q = q_ref[0]
    for p in range(N_HEADS // 2):
        qp = q[:, 128 * p:128 * (p + 1)]
        if use_norm:
            qp = qp * lax.rsqrt(_head_ms(qp) + EPS) * qn_ref[...]
        qp = _rope128(qp, cos, sin) * scale
        g = (2 * p) // (N_HEADS // N_KV)
        other = pltpu.roll(qp, 64, 1)
        if g == 0:
            h0 = jnp.where(lane < 64, qp, 0.0)
            h1 = jnp.where(lane < 64, other, 0.0)
        else:
            h0 = jnp.where(lane >= 64, other, 0.0)
            h1 = jnp.where(lane >= 64, qp, 0.0)
        qz_ref[0, :, 256 * p:256 * p + 128] = h0.astype(BF16)
        qz_ref[0, :, 256 * p + 128:256 * p + 256] = h1.astype(BF16)


def _attn_prep_call(p, cos, sin, qn, kn, q_off, kv_off, use_norm, tm=256):
    B, T, _ = p.shape
    return pl.pallas_call(
        functools.partial(_attn_prep_kernel, use_norm=use_norm),
        grid=(B, T // tm),
        in_specs=[pl.BlockSpec((1, tm, BW), lambda b, i: (b, i, q_off // BW)),
                  pl.BlockSpec((1, tm, 2 * KVW), lambda b, i: (b, i, kv_off // (2 * KVW))),
                  pl.BlockSpec((tm, 128), lambda b, i: (i, 0)),
                  pl.BlockSpec((tm, 128), lambda b, i: (i, 0)),
                  pl.BlockSpec((1, 128), lambda b, i: (0, 0)),
                  pl.BlockSpec((1, 128), lambda b, i: (0, 0))],
        out_specs=[pl.BlockSpec((1, tm, N_HEADS * 128), lambda b, i: (b, i, 0)),
                   pl.BlockSpec((1, tm, KVW), lambda b, i: (b, i, 0)),
                   pl.BlockSpec((1, tm, KVW), lambda b, i: (b, i, 0))],
        out_shape=[jax.ShapeDtypeStruct((B, T, N_HEADS * 128), BF16),
                   jax.ShapeDtypeStruct((B, T, KVW), BF16),
                   jax.ShapeDtypeStruct((B, T, KVW), BF16)],
        compiler_params=_cparams(("parallel", "parallel")),
        name="attn_prep",
    )(p, p, cos, sin, qn, kn)


def _flash_kernel(sink_ref, qz_ref, k_ref, v_ref, g_ref, o_ref, m_ref, l_ref, acc_ref,
                  *, window, tq, tk, n_kv_steps):
    qi = pl.program_id(1)
    ks = pl.program_id(2)

    @pl.when(ks == 0)
    def _():
        for h in range(N_HEADS):
            if window:
                m_ref[h] = jnp.full((tq, 128), sink_ref[h], F32)
                l_ref[h] = jnp.ones((tq, 128), F32)
            else:
                m_ref[h] = jnp.full((tq, 128), NEG, F32)
                l_ref[h] = jnp.zeros((tq, 128), F32)
        acc_ref[...] = jnp.zeros_like(acc_ref)

    def step(mask):
        k = k_ref[0]
        v = v_ref[0]
        for h in range(N_HEADS):
            s = _dot_nt(qz_ref[0, :, 128 * h:128 * (h + 1)], k)
            if mask is not None:
                s = jnp.where(mask, s, NEG)
            m_prev = m_ref[h]
            m_new = jnp.maximum(m_prev, jnp.max(s, axis=-1, keepdims=True))
            alpha = jnp.exp(m_prev - m_new)
            pr = jnp.exp(s - m_new[:, :1])
            if mask is not None:
                pr = jnp.where(mask, pr, 0.0)
            l_ref[h] = alpha * l_ref[h] + jnp.sum(pr, axis=-1, keepdims=True)
            m_ref[h] = m_new
            acc_ref[h] = alpha * acc_ref[h] + _dot(pr.astype(BF16), v)

    row = lax.broadcasted_iota(jnp.int32, (tq, tk), 0)
    col = lax.broadcasted_iota(jnp.int32, (tq, tk), 1)
    if window:
        n_tiles = pl.num_programs(1)

        @pl.when(ks == 0)
        def _():
            step(None)

        kt = qi + ks - 2

        @pl.when((ks > 0) & (qi > 0) & (kt >= 1) & (kt < n_tiles))
        def _():
            d = row - col + (2 - ks) * tq
            step(jnp.abs(d) <= WINDOW)
    else:
        q_has_ctx = qi * tq < CTX

        @pl.when(jnp.logical_not(q_has_ctx))
        def _():
            step(None)

        @pl.when(q_has_ctx)
        def _():
            step((row + qi * tq >= CTX) | (col + ks * tk < CTX))

    @pl.when(ks == n_kv_steps - 1)
    def _():
        lane = lax.broadcasted_iota(jnp.int32, (tq, 128), 1)
        for p in range(N_HEADS // 2):
            g = (2 * p) // (N_HEADS // N_KV)
            a0 = acc_ref[2 * p] / l_ref[2 * p]
            a1 = acc_ref[2 * p + 1] / l_ref[2 * p + 1]
            if g == 0:
                o = jnp.where(lane < 64, a0, pltpu.roll(a1, 64, 1))
            else:
                o = jnp.where(lane < 64, pltpu.roll(a0, 64, 1), a1)
            gate = g_ref[0, :, 128 * p:128 * (p + 1)]
            o_ref[0, :, 128 * p:128 * (p + 1)] = o * (gate * jax.nn.sigmoid(gate))


def _flash_call(sink, qz, k, v, p, g_off, window, tq=256, tk=768):
    B, T, _ = qz.shape
    if window:
        tq = tk = CTX
        n_kv_steps = 4
        n_tiles = T // tq

        def kv_map(b, i, s):
            return (b, jnp.where(s == 0, 0, jnp.clip(i + s - 2, 0, n_tiles - 1)), 0)
    else:
        n_kv_steps = T // tk

        def kv_map(b, i, s):
            return (b, s, 0)

    return pl.pallas_call(
        functools.partial(_flash_kernel, window=window, tq=tq, tk=tk, n_kv_steps=n_kv_steps),
        grid=(B, T // tq, n_kv_steps),
        in_specs=[pl.BlockSpec(memory_space=pltpu.SMEM),
                  pl.BlockSpec((1, tq, N_HEADS * 128), lambda b, i, s: (b, i, 0)),
                  pl.BlockSpec((1, tk, KVW), kv_map),
                  pl.BlockSpec((1, tk, KVW), kv_map),
                  pl.BlockSpec((1, tq, BW), lambda b, i, s: (b, i, g_off // BW))],
        out_specs=pl.BlockSpec((1, tq, BW), lambda b, i, s: (b, i, 0)),
        out_shape=jax.ShapeDtypeStruct((B, T, BW), F32),
        scratch_shapes=[pltpu.VMEM((N_HEADS, tq, 128), F32),
                        pltpu.VMEM((N_HEADS, tq, 128), F32),
                        pltpu.VMEM((N_HEADS, tq, 128), F32)],
        compiler_params=_cparams(("parallel", "parallel", "arbitrary")),
        name="flash_window" if window else "flash_global",
    )(sink, qz, k, v, p)


def _chunk_masks(d):
    row = lax.broadcasted_iota(jnp.int32, (CHUNK, 2 * CHUNK), 0)
    col = lax.broadcasted_iota(jnp.int32, (CHUNK, 2 * CHUNK), 1) % CHUNK
    lag = (row - col) * (1 - 2 * d)
    return row, col, lag > 0, lag >= 0


def _hl(x):
    hi = x.astype(BF16)
    return hi, (x - hi.astype(F32)).astype(BF16)


def _lhs_tok(m2):
    return jnp.concatenate(_hl(m2), axis=1)


def _rhs_tok(x):
    xh, xl = _hl(x)
    return jnp.concatenate([xh, xl, xh, xl], axis=0)


def _mm_tok(m2, x):
    return _dot(_lhs_tok(m2), _rhs_tok(x))


def _mm_nt3(a, b):
    ah, al = _hl(a)
    bh, bl = _hl(b)
    return (_dot_nt(jnp.concatenate([ah, al], axis=1), jnp.concatenate([bh, bh], axis=1))
            + _dot_nt(ah, bl))


def _mm_ch3(a, s):
    ah, al = _hl(a)
    sh, sl = _hl(s)
    return (_dot(jnp.concatenate([ah, al], axis=1), jnp.concatenate([sh, sh], axis=0))
            + _dot(ah, sl))


def _mm_tn(a, b):
    ah, al = _hl(a)
    bh, bl = _hl(b)
    at = jnp.concatenate([ah.astype(F32), al.astype(F32)], axis=0).T.astype(BF16)
    return _dot(jnp.concatenate([at, at], axis=1), jnp.concatenate([bh, bh, bl, bl], axis=0))


def _unit_tri_inverse(a2, row, col):
    n = len(a2)
    eye = (row == col).astype(F32)
    t2 = [eye - jnp.where((row ^ col) == 1, a, 0.0) for a in a2]
    s = 2
    while s < CHUNK:
        off = ((row ^ col) // s) == 1
        hl = [_hl(t) for t in t2]
        y2 = [_dot(_lhs_tok(jnp.where(off, a2[i], 0.0)),
                   jnp.concatenate([hl[i][0], hl[i][1], hl[i][0], hl[i][1]], axis=0)) for i in range(n)]
        z2 = [_dot(jnp.concatenate(hl[i], axis=1), _rhs_tok(y2[i])) for i in range(n)]
        t2 = [t2[i] - z2[i] for i in range(n)]
        s *= 2
    return t2


def _scan_chunk_index(p, d, n_ctx_chunks, n_chunks):
    rev_idx = jnp.where(p < n_ctx_chunks, n_ctx_chunks - 1 - p, n_chunks - 1 + n_ctx_chunks - p)
    return jnp.where(d == 0, p, rev_idx)


def _rwkv_kernel(r_ref, v_ref, kk_ref, lw_ref, a_ref, kt_ref, y_ref, s_ref):
    d = pl.program_id(1)

    @pl.when(pl.program_id(2) == 0)
    def _():
        s_ref[...] = jnp.zeros_like(s_ref)

    C = CHUNK
    row, col, strict, incl = _chunk_masks(d)
    lw = lw_ref[0, 0]
    g = _dot(incl[:, :C].astype(F32), lw, HI)
    g_last = jnp.sum(lw, axis=0, keepdims=True)
    eg = jnp.exp(g)
    egx = jnp.exp(g - lw)
    ieg = jnp.exp(-g)
    egl = jnp.exp(g_last)
    kk = kk_ref[0]
    kp_all = kk * egx
    rg_all = r_ref[0] * eg
    kd_all = kt_ref[0, 0] * ieg
    ad_all = kk * a_ref[0, 0] * ieg
    v_all = v_ref[0]
    lane = lax.broadcasted_iota(jnp.int32, (C, 128), 1)
    even = lane < HD
    r128 = lax.broadcasted_iota(jnp.int32, (128, 128), 0)
    c128 = lax.broadcasted_iota(jnp.int32, (128, 128), 1)
    blockdiag = (r128 // HD) == (c128 // HD)
    NP = N_HEADS // 2
    sls = [slice(128 * p, 128 * (p + 1)) for p in range(NP)]
    kp = [kp_all[:, s] for s in sls]
    rg = [rg_all[:, s] for s in sls]
    kd = [kd_all[:, s] for s in sls]
    ad = [ad_all[:, s] for s in sls]
    v = [v_all[:, s] for s in sls]
    el = [egl[:, s] for s in sls]
    gram = [_mm_nt3(jnp.concatenate([jnp.where(even, kp[p], 0.0), jnp.where(even, rg[p], 0.0),
                                     jnp.where(even, 0.0, kp[p]), jnp.where(even, 0.0, rg[p])], axis=0),
                    jnp.concatenate([kd[p], kd[p], ad[p], ad[p]], axis=0)) for p in range(NP)]
    heads = [(p, e) for p in range(NP) for e in range(2)]
    gk = [gram[p][2 * C * e:2 * C * e + C] for p, e in heads]
    gr = [gram[p][2 * C * e + C:2 * C * (e + 1)] for p, e in heads]
    t2 = _unit_tri_inverse([jnp.where(strict, x[:, 2 * C:], 0.0) for x in gk], row, col)
    v_rhs = [_rhs_tok(x) for x in v]
    xy = [_dot(_lhs_tok(jnp.concatenate([jnp.where(strict, gk[i][:, :2 * C], 0.0),
                                         jnp.where(incl, gr[i][:, :2 * C], 0.0)], axis=0)), v_rhs[p])
          for i, (p, e) in enumerate(heads)]
    sol = [_mm_tok(t2[i], jnp.concatenate([kp[p], xy[i][:C]], axis=1))
           for i, (p, e) in enumerate(heads)]
    s0 = [s_ref[p] for p in range(NP)]
    rd = [_mm_nt3(jnp.concatenate([jnp.where(even, sol[2 * p][:, :128], sol[2 * p + 1][:, :128]), rg[p]],
                                  axis=0), s0[p]) for p in range(NP)]
    u = [jnp.where(even, sol[2 * p][:, 128:], sol[2 * p + 1][:, 128:]) + rd[p][:C] for p in range(NP)]
    u_rhs = [_rhs_tok(x) for x in u]
    bu = [_dot(_lhs_tok(jnp.where(incl, gr[i][:, 2 * C:], 0.0)), u_rhs[p]) for i, (p, e) in enumerate(heads)]
    for p in range(NP):
        y_ref[0, 0, :, sls[p]] = (rd[p][C:] + jnp.where(even, xy[2 * p][C:], xy[2 * p + 1][C:])
                                  - jnp.where(even, bu[2 * p], bu[2 * p + 1]))
    upd = [_mm_tn(v[p], kd[p] * el[p]) - _mm_tn(u[p], ad[p] * el[p]) for p in range(NP)]
    for p in range(NP):
        s_ref[p] = s0[p] * el[p] + jnp.where(blockdiag, upd[p], 0.0)


def _rwkv_call(r, v, kk, lw, a, kt):
    B, T, W = r.shape
    nc = T // CHUNK
    ncc = CTX // CHUNK

    def tok(b, d, p):
        return (b, _scan_chunk_index(p, d, ncc, nc), 0)

    def tokd(b, d, p):
        return (d, b, _scan_chunk_index(p, d, ncc, nc), 0)

    return pl.pallas_call(
        _rwkv_kernel,
        grid=(B, 2, nc),
        in_specs=[pl.BlockSpec((1, CHUNK, W), tok)] * 3 + [pl.BlockSpec((1, 1, CHUNK, W), tokd)] * 3,
        out_specs=pl.BlockSpec((1, 1, CHUNK, W), tokd),
        out_shape=jax.ShapeDtypeStruct((2, B, T, W), F32),
        scratch_shapes=[pltpu.VMEM((N_HEADS // 2, 128, 128), F32)],
        compiler_params=_cparams(("parallel", "parallel", "arbitrary")),
        name="rwkv_scan",
    )(r, v, kk, lw, a, kt)


def _gdn_kernel(q_ref, k_ref, v_ref, lg_ref, beta_ref, o_ref, s_ref):
    d = pl.program_id(1)

    @pl.when(pl.program_id(2) == 0)
    def _():
        s_ref[...] = jnp.zeros_like(s_ref)

    C = CHUNK
    row, col, strict, incl = _chunk_masks(d)
    lg_all = lg_ref[0, 0]
    beta_all = beta_ref[0, 0]
    gc_all = _dot(incl[:, :C].astype(F32), lg_all, HI)
    incl_t = ((row - col) * (1 - 2 * d) <= 0).astype(F32)
    gct_all = _dot_tn(lg_all, incl_t, HI)
    gl_all = jnp.sum(lg_all, axis=0, keepdims=True)
    H = range(D_HEADS)
    sls = [slice(D_HD * h, D_HD * (h + 1)) for h in H]
    q = [q_ref[0, :, s] for s in sls]
    k = [k_ref[0, :, s] for s in sls]
    v = [v_ref[0, :, s] for s in sls]
    gc = [gc_all[:, h:h + 1] for h in H]
    beta = [beta_all[:, h:h + 1] for h in H]
    gl = [gl_all[:, h:h + 1] for h in H]
    dmask = [jnp.where(incl, jnp.exp(jnp.where(incl, gc[h] - gct_all[h:h + 1, :], 0.0)), 0.0) for h in H]
    kb = [k[h] * beta[h] for h in H]
    eg = [jnp.exp(gc[h]) for h in H]
    gram = [_mm_nt3(jnp.concatenate([kb[h], q[h]], axis=0), jnp.concatenate([k[h], k[h]], axis=0)) for h in H]
    t2 = _unit_tri_inverse([jnp.where(strict, gram[h][:C] * dmask[h], 0.0) for h in H], row, col)
    sol = [_mm_tok(t2[h], jnp.concatenate([v[h] * beta[h], kb[h] * eg[h]], axis=1)) for h in H]
    s0 = [s_ref[h] for h in H]
    rd = [_mm_ch3(jnp.concatenate([sol[h][:, D_HD:], q[h] * eg[h]], axis=0), s0[h]) for h in H]
    vnew = [sol[h][:, :D_HD] - rd[h][:C] for h in H]
    qv = [_mm_tok(gram[h][C:] * dmask[h], vnew[h]) for h in H]
    for h in H:
        o_ref[0, 0, :, sls[h]] = rd[h][C:] + qv[h]
    upd = [_mm_tn(k[h] * jnp.exp(gl[h] - gc[h]), vnew[h]) for h in H]
    for h in H:
        s_ref[h] = s0[h] * jnp.exp(gl[h]) + upd[h]


def _gdn_call(q, k, v, lg, beta):
    B, T, W = q.shape
    nc = T // CHUNK
    ncc = CTX // CHUNK

    def tok(b, d, p):
        return (b, _scan_chunk_index(p, d, ncc, nc), 0)

    def tokd(b, d, p):
        return (d, b, _scan_chunk_index(p, d, ncc, nc), 0)

    return pl.pallas_call(
        _gdn_kernel,
        grid=(B, 2, nc),
        in_specs=[pl.BlockSpec((1, CHUNK, W), tok)] * 3 + [pl.BlockSpec((1, 1, CHUNK, 128), tokd)] * 2,
        out_specs=pl.BlockSpec((1, 1, CHUNK, W), tokd),
        out_shape=jax.ShapeDtypeStruct((2, B, T, W), F32),
        scratch_shapes=[pltpu.VMEM((D_HEADS, D_HD, D_HD), F32)],
        compiler_params=_cparams(("parallel", "parallel", "arbitrary")),
        name="gdn_scan",
    )(q, k, v, lg, beta)


def _branch_kernel(pm_ref, ya_ref, yb_ref, yc_ref, yd_ref, gup_ref, gb_ref, wbr_ref, o_ref):
    pm = pm_ref[0].astype(BF16)
    acc = None
    for i, y_ref in enumerate((ya_ref, yb_ref, yc_ref, yd_ref)):
        gate = jax.nn.sigmoid(_dot(pm, gup_ref[i]) + gb_ref[i])
        term = gate * _dot(y_ref[0].astype(BF16), wbr_ref[i])
        acc = term if acc is None else acc + term
    o_ref[0] = acc.astype(BF16)


def _branch_call(p, ys, g_up, g_b, w_br, tm=256):
    B, T, _ = p.shape
    D = D_MODEL
    tok = lambda b, i: (b, i, 0)
    return pl.pallas_call(
        _branch_kernel,
        grid=(B, T // tm),
        in_specs=[pl.BlockSpec((1, tm, GATE_RANK), lambda b, i: (b, i, P_G // GATE_RANK))]
        + [pl.BlockSpec((1, tm, BW), tok)] * 4
        + [pl.BlockSpec((N_BRANCH, GATE_RANK, D), lambda b, i: (0, 0, 0)),
           pl.BlockSpec((N_BRANCH, 1, D), lambda b, i: (0, 0, 0)),
           pl.BlockSpec((N_BRANCH, BW, D), lambda b, i: (0, 0, 0))],
        out_specs=pl.BlockSpec((1, tm, D), tok),
        out_shape=jax.ShapeDtypeStruct((B, T, D), BF16),
        compiler_params=_cparams(("parallel", "parallel")),
        name="branch_merge",
    )(p, *ys, g_up, g_b.reshape(N_BRANCH, 1, D), w_br)


def _outproj_kernel(x_ref, acc_ref, w_ref, mod_ref, fg_ref, o_ref, *, tm, tile_off, final):
    row = (pl.program_id(1) + tile_off) * tm + lax.broadcasted_iota(jnp.int32, (tm, 1), 0)
    m = mod_ref[0]
    gate = jnp.where(row < CTX, m[5:6], m[4:5])
    y = x_ref[0] + gate * _dot(acc_ref[0], w_ref[...])
    if final:
        ms = jnp.mean(y * y, axis=-1, keepdims=True)
        y = y * lax.rsqrt(ms + EPS) * fg_ref[...]
    o_ref[0] = y


def _outproj_call(x, acc, w_out, modv, final_g, final, tm=256):
    B, T, D = x.shape
    tile_off = CTX // tm if final else 0
    t_out = T - CTX if final else T
    tok_in = lambda b, i: (b, i + tile_off, 0)
    return pl.pallas_call(
        functools.partial(_outproj_kernel, tm=tm, tile_off=tile_off, final=final),
        grid=(B, t_out // tm),
        in_specs=[pl.BlockSpec((1, tm, D), tok_in),
                  pl.BlockSpec((1, tm, D), tok_in),
                  pl.BlockSpec((D, D), lambda b, i: (0, 0)),
                  pl.BlockSpec((1, 8, D), lambda b, i: (b, 0, 0)),
                  pl.BlockSpec((1, D), lambda b, i: (0, 0))],
        out_specs=pl.BlockSpec((1, tm, D), lambda b, i: (b, i, 0)),
        out_shape=jax.ShapeDtypeStruct((B, t_out, D), F32),
        compiler_params=_cparams(("parallel", "parallel")),
        name="outproj",
    )(x, acc, w_out, modv, final_g.reshape(1, D))


def _pad_w_in(w):
    oa, ob, oc, od = 0, 1280, 3456, 4736
    og = od + 2064
    D = w.shape[0]
    z = lambda n: jnp.zeros((D, n), w.dtype)
    a_q, a_kv, a_g = w[:, oa:oa + 512], w[:, oa + 512:oa + 768], w[:, oa + 768:oa + 1280]
    b_rkv, b_l, b_g = w[:, ob:ob + 1536], w[:, ob + 1536:ob + 1664], w[:, ob + 1664:ob + 2176]
    c_q, c_kv, c_g = w[:, oc:oc + 512], w[:, oc + 512:oc + 768], w[:, oc + 768:oc + 1280]
    d_qkv, d_ab, d_g = w[:, od:od + 1536], w[:, od + 1536:od + 1552], w[:, od + 1552:od + 2064]
    g = w[:, og:og + GATE_RANK]
    cols = [a_q, a_g, c_q, c_g, b_g, d_g, b_rkv, d_qkv, a_kv, c_kv, g, b_l, d_ab, z(112)]
    return jnp.concatenate(cols, axis=1).astype(BF16)


def _rope_tables(n):
    rows = n // GRID_W
    row = jnp.repeat(jnp.arange(rows, dtype=F32), GRID_W)
    col = jnp.tile(jnp.arange(GRID_W, dtype=F32), rows)
    half = HD // 2
    inv = ROPE_THETA ** (-jnp.arange(0, half, 2, dtype=F32) / half)
    ar = row[:, None] * inv
    ac = col[:, None] * inv
    cos = jnp.concatenate([jnp.cos(ar), jnp.cos(ar), jnp.cos(ac), jnp.cos(ac)], axis=-1)
    sin = jnp.concatenate([-jnp.sin(ar), jnp.sin(ar), -jnp.sin(ac), jnp.sin(ac)], axis=-1)
    cos = jnp.concatenate([jnp.ones((CTX, HD), F32), cos], axis=0)
    sin = jnp.concatenate([jnp.zeros((CTX, HD), F32), sin], axis=0)
    return jnp.tile(cos, (1, 2)), jnp.tile(sin, (1, 2))


def _seg_shift(z, k):
    def sh(a):
        if k > 0:
            return jnp.pad(a[:, :-k], ((0, 0), (k, 0), (0, 0)))
        return jnp.pad(a[:, -k:], ((0, 0), (0, -k), (0, 0)))
    return jnp.concatenate([sh(z[:, :CTX]), sh(z[:, CTX:])], axis=1)


def _group_sum(x, hd):
    s = x.reshape(x.shape[:-1] + (x.shape[-1] // hd, hd)).sum(-1, keepdims=True)
    return jnp.broadcast_to(s, s.shape[:-1] + (hd,)).reshape(x.shape)


def _silu(x):
    return x * jax.nn.sigmoid(x)


def _mixer_b(p, mu, w0, wup, a0, aup, k_k, k_a, r_k, ln_g, ln_b):
    W = BW
    z = jnp.concatenate([p[..., P_BRKV:P_BRKV + 3 * W], p[..., P_BL:P_BL + 2 * B_LORA]], axis=-1)
    g = p[..., P_BG:P_BG + W]
    z = z + mu[0] * (_seg_shift(z, 1) - z) + mu[1] * (_seg_shift(z, -1) - z)
    r, k, v = z[..., :W], z[..., W:2 * W], z[..., 2 * W:3 * W]
    wl = jnp.tanh(z[..., 3 * W:3 * W + B_LORA])
    al = z[..., 3 * W + B_LORA:]
    kk = k * k_k
    kk = kk * lax.rsqrt(jnp.maximum(_group_sum(kk * kk, HD), EPS * EPS))
    lws, as_, kts, bonus = [], [], [], 0.0
    rk = r_k.reshape(W)
    for d in range(2):
        lw = -B_DECAY_SCALE * jax.nn.sigmoid(w0[d] + jnp.dot(wl, wup[d], precision=HI))
        a = jax.nn.sigmoid(a0[d] + jnp.dot(al, aup[d], precision=HI))
        kt = k * (1.0 + (a - 1.0) * k_a)
        lws.append(lw)
        as_.append(a)
        kts.append(kt)
        bonus = bonus + _group_sum(r * kt * rk, HD) * v
    y = _rwkv_call(r, v, kk, jnp.stack(lws), jnp.stack(as_), jnp.stack(kts))
    wkv = y[0] + y[1]
    mean = _group_sum(wkv, HD) / HD
    cen = wkv - mean
    var = _group_sum(cen * cen, HD) / HD
    gn = cen * lax.rsqrt(var + B_GN_EPS) * ln_g + ln_b
    return (gn + bonus) * _silu(g)


def _mixer_d(p, conv_w, alog, dtb, norm_g):
    W, H = BW, D_HEADS
    zin = p[..., P_DQKV:P_DQKV + 3 * W]
    g = p[..., P_DG:P_DG + W]
    conv = sum(conv_w[i] * _seg_shift(zin, D_CONV // 2 - i) if i != D_CONV // 2 else conv_w[i] * zin
               for i in range(D_CONV))
    qkv = _silu(conv)
    q, k, v = qkv[..., :W], qkv[..., W:2 * W], qkv[..., 2 * W:]
    q = q * lax.rsqrt(jnp.maximum(_group_sum(q * q, D_HD), EPS * EPS)) * (D_HD ** -0.5)
    k = k * lax.rsqrt(jnp.maximum(_group_sum(k * k, D_HD), EPS * EPS))
    ab = p[..., P_DAB:P_DAB + 4 * H]
    lgs, betas = [], []
    for d in range(2):
        a_raw = ab[..., 2 * d * H:(2 * d + 1) * H]
        b_raw = ab[..., (2 * d + 1) * H:(2 * d + 2) * H]
        lg = -jnp.exp(alog[d]) * jax.nn.softplus(a_raw + dtb[d])
        pad = ((0, 0), (0, 0), (0, 128 - H))
        lgs.append(jnp.pad(lg, pad))
        betas.append(jnp.pad(jax.nn.sigmoid(b_raw), pad))
    o = _gdn_call(q, k, v, jnp.stack(lgs), jnp.stack(betas))
    o = o[0] + o[1]
    ms = _group_sum(o * o, D_HD) / D_HD
    return o * lax.rsqrt(ms + EPS) * jnp.tile(norm_g, H) * _silu(g)


def kernel(x, c, ctx, c_ctx, norm_g, w_mod, b_mod, w_in, a_sink, b_mu, b_w0, b_wup, b_a0, b_aup,
           b_kk, b_ka, b_rk, b_lng, b_lnb, c_qn, c_kn, d_conv, d_alog, d_dtb, d_norm, g_up, g_b,
           w_br, w_out, final_g):
    B, n, D = x.shape
    L = w_mod.shape[0]
    T = CTX + n
    xa = jnp.concatenate([ctx, x], axis=1)
    cs = jnp.zeros((8, D), F32).at[:B].set(c).at[B].set(c_ctx)
    mod = _mod_call(cs, w_mod, b_mod)
    cos, sin = _rope_tables(n)
    ones = jnp.ones((1, 128), F32)
    tm_in = 768 if T % 768 == 0 else 256
    for l in range(L):
        shift, scale, gate = jnp.split(mod[l], 3, axis=-1)
        bc = lambda v: jnp.broadcast_to(v[B], (B, D))
        modv = jnp.stack([scale[:B], shift[:B], bc(scale), bc(shift), gate[:B], bc(gate),
                          jnp.zeros((B, D), F32), jnp.zeros((B, D), F32)], axis=1)
        p = _inproj_call(xa, modv, norm_g[l], _pad_w_in(w_in[l]), tm=tm_in)
        qz, k, v = _attn_prep_call(p, cos, sin, ones, ones, P_AQ, P_AKV, False)
        y_a = _flash_call(a_sink[l], qz, k, v, p, P_AG, True)
        qz, k, v = _attn_prep_call(p, cos, sin, jnp.tile(c_qn[l], 2)[None], jnp.tile(c_kn[l], 2)[None],
                                   P_CQ, P_CKV, True)
        y_c = _flash_call(a_sink[l], qz, k, v, p, P_CG, False, tk=tm_in)
        y_b = _mixer_b(p, b_mu[l], b_w0[l], b_wup[l], b_a0[l], b_aup[l], b_kk[l], b_ka[l], b_rk[l],
                       b_lng[l], b_lnb[l])
        y_d = _mixer_d(p, d_conv[l], d_alog[l], d_dtb[l], d_norm[l])
        acc = _branch_call(p, (y_a, y_b, y_c, y_d), g_up[l].astype(BF16), g_b[l], w_br[l].astype(BF16))
        xa = _outproj_call(xa, acc, w_out[l].astype(BF16), modv, final_g, l == L - 1)
    return xa
```

```python
import functools
import math

import jax
import jax.numpy as jnp
from jax import lax
from jax.experimental import pallas as pl
from jax.experimental.pallas import tpu as pltpu

F32 = jnp.float32
BF16 = jnp.bfloat16
HI = lax.Precision.HIGHEST

D_MODEL = 2048
DEPTH = 4
GRID_W = 64
CTX = 256
N_BRANCH = 4
BW = D_MODEL // 4
HD = 64
N_HEADS = BW // HD
N_KV = 2
KVW = N_KV * HD
WINDOW = 128
B_LORA = 64
B_DECAY_SCALE = 0.606531
B_GN_EPS = 64e-5
D_HD = 128
D_HEADS = BW // D_HD
D_CONV = 5
GATE_RANK = D_MODEL // 8
ROPE_THETA = 10000.0
EPS = 1e-6
NEG = -1e30
CHUNK = 64

P_AQ, P_AG, P_CQ, P_CG, P_BG, P_DG = 0, 512, 1024, 1536, 2048, 2560
P_BRKV, P_DQKV = 3072, 4608
P_AKV, P_CKV, P_G = 6144, 6400, 6656
P_BL, P_DAB = 6912, 7040
P_W = 7168

VMEM_LIMIT = 56 * 1024 * 1024


def _cparams(sem):
    return pltpu.CompilerParams(dimension_semantics=sem, vmem_limit_bytes=VMEM_LIMIT)


def _dot(a, b, prec=None):
    return jnp.dot(a, b, precision=prec, preferred_element_type=F32)


def _dot_nt(a, b, prec=None):
    return lax.dot_general(a, b, (((1,), (1,)), ((), ())), precision=prec,
                           preferred_element_type=F32)


def _dot_tn(a, b, prec=None):
    return lax.dot_general(a, b, (((0,), (0,)), ((), ())), precision=prec,
                           preferred_element_type=F32)


def _mod_kernel(c_ref, w_ref, b_ref, o_ref):
    c = c_ref[...]
    s = c * jax.nn.sigmoid(c)
    o_ref[0] = _dot(s, w_ref[0], HI) + b_ref[0]


def _mod_call(cs, w_mod, b_mod):
    L, D, D3 = w_mod.shape
    tn = 768
    return pl.pallas_call(
        _mod_kernel,
        grid=(L, D3 // tn),
        in_specs=[pl.BlockSpec((8, D), lambda l, j: (0, 0)),
                  pl.BlockSpec((1, D, tn), lambda l, j: (l, 0, j)),
                  pl.BlockSpec((1, 1, tn), lambda l, j: (l, 0, j))],
        out_specs=pl.BlockSpec((1, 8, tn), lambda l, j: (l, 0, j)),
        out_shape=jax.ShapeDtypeStruct((L, 8, D3), F32),
        compiler_params=_cparams(("parallel", "parallel")),
        name="mod",
    )(cs, w_mod, b_mod.reshape(L, 1, D3))


def _inproj_kernel(x_ref, mod_ref, g_ref, w_ref, o_ref, h_ref, *, tm):
    @pl.when(pl.program_id(2) == 0)
    def _():
        x = x_ref[0]
        ms = jnp.mean(x * x, axis=-1, keepdims=True)
        y = x * lax.rsqrt(ms + EPS) * g_ref[...]
        row = pl.program_id(1) * tm + lax.broadcasted_iota(jnp.int32, (tm, 1), 0)
        is_ctx = row < CTX
        m = mod_ref[0]
        scale = jnp.where(is_ctx, m[2:3], m[0:1])
        shift = jnp.where(is_ctx, m[3:4], m[1:2])
        h_ref[...] = (y * (1.0 + scale) + shift).astype(BF16)

    o_ref[0] = _dot(h_ref[...], w_ref[...])


def _inproj_call(x, modv, norm_g, w_pad, tm=768, tn=512):
    B, T, D = x.shape
    return pl.pallas_call(
        functools.partial(_inproj_kernel, tm=tm),
        grid=(B, T // tm, P_W // tn),
        in_specs=[pl.BlockSpec((1, tm, D), lambda b, i, j: (b, i, 0)),
                  pl.BlockSpec((1, 8, D), lambda b, i, j: (b, 0, 0)),
                  pl.BlockSpec((1, D), lambda b, i, j: (0, 0)),
                  pl.BlockSpec((D, tn), lambda b, i, j: (0, j))],
        out_specs=pl.BlockSpec((1, tm, tn), lambda b, i, j: (b, i, j)),
        out_shape=jax.ShapeDtypeStruct((B, T, P_W), F32),
        scratch_shapes=[pltpu.VMEM((tm, D), BF16)],
        compiler_params=_cparams(("parallel", "parallel", "arbitrary")),
        name="inproj",
    )(x, modv, norm_g.reshape(1, D), w_pad)


def _rope128(x, cos, sin_signed):
    lane = lax.broadcasted_iota(jnp.int32, x.shape, 1)
    lo = (lane % 32) < 16
    rot = jnp.where(lo, pltpu.roll(x, 128 - 16, 1), pltpu.roll(x, 16, 1))
    return x * cos + rot * sin_signed


def _head_ms(x):
    r = lax.broadcasted_iota(jnp.int32, (128, 128), 0) // HD
    c = lax.broadcasted_iota(jnp.int32, (128, 128), 1) // HD
    ones = jnp.where(r == c, 1.0 / HD, 0.0).astype(F32)
    return _dot(x * x, ones, HI)


def _attn_prep_kernel(q_ref, kv_ref, cos_ref, sin_ref, qn_ref, kn_ref, qz_ref, k_ref, v_ref,
                      *, use_norm, v_transposed):
    cos = cos_ref[...]
    sin = sin_ref[...]
    tm = cos.shape[0]
    lane = lax.broadcasted_iota(jnp.int32, (tm, 128), 1)
    scale = HD ** -0.5 * LOG2E
    kv = kv_ref[0]
    k = kv[:, :KVW]
    if use_norm:
        k = k * lax.rsqrt(_head_ms(k) + EPS) * kn_ref[...]
    k_ref[0] = _rope128(k, cos, sin).astype(BF16)
    if v_transposed:
        v_ref[0, 0] = kv[:, KVW:].T.astype(BF16)
    else:
        v_ref[0] = kv[:, KVW:].astype(BF16)
    q = q_ref[0]
    for p in range(N_HEADS // 2):
        qp = q[:, 128 * p:128 * (p + 1)]
        if use_norm:
            qp = qp * lax.rsqrt(_head_ms(qp) + EPS) * qn_ref[...]
        qp = _rope128(qp, cos, sin) * scale
        g = (2 * p) // (N_HEADS // N_KV)
        other = pltpu.roll(qp, 64, 1)
        if g == 0:
            h0 = jnp.where(lane < 64, qp, 0.0)
            h1 = jnp.where(lane < 64, other, 0.0)
        else:
            h0 = jnp.where(lane >= 64, other, 0.0)
            h1 = jnp.where(lane >= 64, qp, 0.0)
        qz_ref[0, :, 256 * p:256 * p + 128] = h0.astype(BF16)
        qz_ref[0, :, 256 * p + 128:256 * p + 256] = h1.astype(BF16)


def _attn_prep_call(p, cos, sin, qn, kn, q_off, kv_off, use_norm, v_transposed):
    B, T, _ = p.shape
    tm = KC
    if v_transposed:
        v_spec = pl.BlockSpec((1, 1, KVW, tm), lambda b, i: (b, i, 0, 0))
        v_shape = jax.ShapeDtypeStruct((B, T // tm, KVW, tm), BF16)
    else:
        v_spec = pl.BlockSpec((1, tm, KVW), lambda b, i: (b, i, 0))
        v_shape = jax.ShapeDtypeStruct((B, T, KVW), BF16)
    return pl.pallas_call(
        functools.partial(_attn_prep_kernel, use_norm=use_norm, v_transposed=v_transposed),
        grid=(B, T // tm),
        in_specs=[pl.BlockSpec((1, tm, BW), lambda b, i: (b, i, q_off // BW)),
                  pl.BlockSpec((1, tm, 2 * KVW), lambda b, i: (b, i, kv_off // (2 * KVW))),
                  pl.BlockSpec((tm, 128), lambda b, i: (i, 0)),
                  pl.BlockSpec((tm, 128), lambda b, i: (i, 0)),
                  pl.BlockSpec((1, 128), lambda b, i: (0, 0)),
                  pl.BlockSpec((1, 128), lambda b, i: (0, 0))],
        out_specs=[pl.BlockSpec((1, tm, N_HEADS * 128), lambda b, i: (b, i, 0)),
                   pl.BlockSpec((1, tm, KVW), lambda b, i: (b, i, 0)),
                   v_spec],
        out_shape=[jax.ShapeDtypeStruct((B, T, N_HEADS * 128), BF16),
                   jax.ShapeDtypeStruct((B, T, KVW), BF16),
                   v_shape],
        compiler_params=_cparams(("parallel", "parallel")),
        name="attn_prep",
    )(p, p, cos, sin, qn, kn)


LOG2E = 1.4426950408889634
KC = 256


def _flash_w_kernel(sink_ref, qz_ref, k_ref, v_ref, g_ref, o_ref, m_ref, l_ref, acc_ref, *, tq):
    qi = pl.program_id(1)
    T = k_ref.shape[1]

    for h in range(N_HEADS):
        m_ref[h] = jnp.full((tq, 128), sink_ref[h] * LOG2E, F32)
        l_ref[h] = jnp.ones((tq, 128), F32)
    acc_ref[...] = jnp.zeros_like(acc_ref)

    def update(starts, masks):
        H = range(N_HEADS)
        ks = [k_ref[0, pl.ds(s, KC), :] for s in starts]
        vs = [v_ref[0, pl.ds(s, KC), :] for s in starts]
        s = [[_dot_nt(qz_ref[0, :, 128 * h:128 * (h + 1)], kc) for kc in ks] for h in H]
        m_new, alpha = [], []
        for h in H:
            mx = None
            for i, mk in enumerate(masks):
                if mk is not None:
                    s[h][i] = jnp.where(mk, s[h][i], NEG)
                mx = s[h][i] if mx is None else jnp.maximum(mx, s[h][i])
            m_prev = m_ref[h]
            m_new.append(jnp.maximum(m_prev, jnp.max(mx, axis=-1, keepdims=True)))
            alpha.append(jnp.exp2(m_prev - m_new[h]))
        for h in H:
            m_wide = jnp.concatenate([m_new[h]] * (KC // 128), axis=1)
            lsum = None
            pv = None
            for sc, vc in zip(s[h], vs):
                pc = jnp.exp2(sc - m_wide)
                lsum = pc if lsum is None else lsum + pc
                d = _dot(pc.astype(BF16), vc)
                pv = d if pv is None else pv + d
            m_ref[h] = m_new[h]
            l_ref[h] = alpha[h] * l_ref[h] + jnp.sum(lsum, axis=-1, keepdims=True)
            acc_ref[h] = alpha[h] * acc_ref[h] + pv

    ctx_starts = [KC * c for c in range(CTX // KC)]

    @pl.when(qi == 0)
    def _():
        update(ctx_starts, [None] * len(ctx_starts))

    @pl.when(qi > 0)
    def _():
        row = lax.broadcasted_iota(jnp.int32, (tq, KC), 0)
        col = lax.broadcasted_iota(jnp.int32, (tq, KC), 1)
        t0 = qi * tq
        starts, masks = [], []
        for c in range((tq + 2 * WINDOW) // KC):
            k0 = t0 - WINDOW + KC * c
            k0c = jnp.minimum(k0, T - KC)
            kpos = k0c + col
            dist = t0 + row - kpos
            masks.append((jnp.abs(dist) <= WINDOW) & (kpos >= jnp.maximum(k0, CTX)))
            starts.append(pl.multiple_of(k0c, WINDOW))
        update(ctx_starts + starts, [None] * len(ctx_starts) + masks)

    lane = lax.broadcasted_iota(jnp.int32, (tq, 128), 1)
    for p in range(N_HEADS // 2):
        g = (2 * p) // (N_HEADS // N_KV)
        a0 = acc_ref[2 * p] / l_ref[2 * p]
        a1 = acc_ref[2 * p + 1] / l_ref[2 * p + 1]
        if g == 0:
            o = jnp.where(lane < 64, a0, pltpu.roll(a1, 64, 1))
        else:
            o = jnp.where(lane < 64, pltpu.roll(a0, 64, 1), a1)
        gate = g_ref[0, :, 128 * p:128 * (p + 1)]
        o_ref[0, :, 128 * p:128 * (p + 1)] = o * (gate * jax.nn.sigmoid(gate))


def _flash_w_call(sink, qz, k, v, p, g_off):
    B, T, _ = qz.shape
    tq = CTX
    return pl.pallas_call(
        functools.partial(_flash_w_kernel, tq=tq),
        grid=(B, T // tq),
        in_specs=[pl.BlockSpec(memory_space=pltpu.SMEM),
                  pl.BlockSpec((1, tq, N_HEADS * 128), lambda b, i: (b, i, 0)),
                  pl.BlockSpec((1, T, KVW), lambda b, i: (b, 0, 0)),
                  pl.BlockSpec((1, T, KVW), lambda b, i: (b, 0, 0)),
                  pl.BlockSpec((1, tq, BW), lambda b, i: (b, i, g_off // BW))],
        out_specs=pl.BlockSpec((1, tq, BW), lambda b, i: (b, i, 0)),
        out_shape=jax.ShapeDtypeStruct((B, T, BW), F32),
        scratch_shapes=[pltpu.VMEM((N_HEADS, tq, 128), F32),
                        pltpu.VMEM((N_HEADS, tq, 128), F32),
                        pltpu.VMEM((N_HEADS, tq, 128), F32)],
        compiler_params=_cparams(("parallel", "parallel")),
        name="flash_window",
    )(sink, qz, k, v, p)


def _flash_t_kernel(qz_ref, k_ref, vt_ref, g_ref, o_ref, m_ref, l_ref, acc_ref, *, tq, tk):
    qi = pl.program_id(1)
    T = k_ref.shape[1]
    per = N_HEADS // N_KV

    m_ref[...] = jnp.full(m_ref.shape, NEG, F32)
    l_ref[...] = jnp.zeros(l_ref.shape, F32)
    acc_ref[...] = jnp.zeros_like(acc_ref)

    def update(chunks):
        H = range(N_HEADS)
        ks = [k_ref[0, pl.ds(pl.multiple_of(c * KC, KC), KC), :] for c in chunks]
        s = [[_dot_nt(kc, qz_ref[0, :, 128 * h:128 * (h + 1)]) for kc in ks] for h in H]
        m_new, alpha = [], []
        for h in H:
            mx = s[h][0]
            for sc in s[h][1:]:
                mx = jnp.maximum(mx, sc)
            m_prev = m_ref[h, 0:1]
            m_new.append(jnp.maximum(m_prev, jnp.max(mx, axis=0, keepdims=True)))
            alpha.append(jnp.exp2(m_prev - m_new[h]))
        for h in H:
            g = h // per
            lsum = None
            pv = None
            for sc, c in zip(s[h], chunks):
                pc = jnp.exp2(sc - m_new[h])
                lsum = pc if lsum is None else lsum + pc
                d = _dot(vt_ref[0, c, HD * g:HD * (g + 1), :], pc.astype(BF16))
                pv = d if pv is None else pv + d
            m_ref[h] = jnp.broadcast_to(m_new[h], (8, tq))
            l_ref[h] = jnp.broadcast_to(alpha[h] * l_ref[h, 0:1] + jnp.sum(lsum, axis=0, keepdims=True), (8, tq))
            acc_ref[h] = alpha[h] * acc_ref[h] + pv

    @pl.when(qi == 0)
    def _():
        update([c for c in range(CTX // KC)])

    @pl.when(qi > 0)
    def _():
        def body(t, carry):
            update([t * (tk // KC) + c for c in range(tk // KC)])
            return carry
        lax.fori_loop(0, T // tk, body, 0)

    o_t = jnp.concatenate([acc_ref[h] / l_ref[h, 0:1] for h in range(N_HEADS)], axis=0)
    gate = g_ref[0]
    o_ref[0] = o_t.T * (gate * jax.nn.sigmoid(gate))


def _flash_t_call(qz, k, vt, p, g_off, tk=768):
    B, T, _ = qz.shape
    tq = CTX
    if T % tk:
        tk = KC
    return pl.pallas_call(
        functools.partial(_flash_t_kernel, tq=tq, tk=tk),
        grid=(B, T // tq),
        in_specs=[pl.BlockSpec((1, tq, N_HEADS * 128), lambda b, i: (b, i, 0)),
                  pl.BlockSpec((1, T, KVW), lambda b, i: (b, 0, 0)),
                  pl.BlockSpec((1, T // KC, KVW, KC), lambda b, i: (b, 0, 0, 0)),
                  pl.BlockSpec((1, tq, BW), lambda b, i: (b, i, g_off // BW))],
        out_specs=pl.BlockSpec((1, tq, BW), lambda b, i: (b, i, 0)),
        out_shape=jax.ShapeDtypeStruct((B, T, BW), F32),
        scratch_shapes=[pltpu.VMEM((N_HEADS, 8, tq), F32),
                        pltpu.VMEM((N_HEADS, 8, tq), F32),
                        pltpu.VMEM((N_HEADS, HD, tq), F32)],
        compiler_params=_cparams(("parallel", "parallel")),
        name="flash_global",
    )(qz, k, vt, p)


def _chunk_masks(d):
    row = lax.broadcasted_iota(jnp.int32, (CHUNK, 2 * CHUNK), 0)
    col = lax.broadcasted_iota(jnp.int32, (CHUNK, 2 * CHUNK), 1) % CHUNK
    lag = (row - col) * (1 - 2 * d)
    return row, col, lag > 0, lag >= 0


def _hl(x):
    hi = x.astype(BF16)
    return hi, (x - hi.astype(F32)).astype(BF16)


def _lhs_tok(m2):
    return jnp.concatenate(_hl(m2), axis=1)


def _rhs_tok(x):
    xh, xl = _hl(x)
    return jnp.concatenate([xh, xl, xh, xl], axis=0)


def _mm_tok(m2, x):
    return _dot(_lhs_tok(m2), _rhs_tok(x))


def _mm_nt3(a, b):
    ah, al = _hl(a)
    bh, bl = _hl(b)
    return (_dot_nt(jnp.concatenate([ah, al], axis=1), jnp.concatenate([bh, bh], axis=1))
            + _dot_nt(ah, bl))


def _mm_ch3(a, s):
    ah, al = _hl(a)
    sh, sl = _hl(s)
    return (_dot(jnp.concatenate([ah, al], axis=1), jnp.concatenate([sh, sh], axis=0))
            + _dot(ah, sl))


def _mm_tn(a, b):
    ah, al = _hl(a)
    bh, bl = _hl(b)
    at = jnp.concatenate([ah.astype(F32), al.astype(F32)], axis=0).T.astype(BF16)
    return _dot(jnp.concatenate([at, at], axis=1), jnp.concatenate([bh, bh, bl, bl], axis=0))


def _unit_tri_inverse(a2, row, col):
    n = len(a2)
    eye = (row == col).astype(F32)
    t2 = [eye - jnp.where((row ^ col) == 1, a, 0.0) for a in a2]
    s = 2
    while s < CHUNK:
        off = ((row ^ col) // s) == 1
        hl = [_hl(t) for t in t2]
        y2 = [_dot(_lhs_tok(jnp.where(off, a2[i], 0.0)),
                   jnp.concatenate([hl[i][0], hl[i][1], hl[i][0], hl[i][1]], axis=0)) for i in range(n)]
        z2 = [_dot(jnp.concatenate(hl[i], axis=1), _rhs_tok(y2[i])) for i in range(n)]
        t2 = [t2[i] - z2[i] for i in range(n)]
        s *= 2
    return t2


def _scan_chunk_index(p, d, n_ctx_chunks, n_chunks):
    rev_idx = jnp.where(p < n_ctx_chunks, n_ctx_chunks - 1 - p, n_chunks - 1 + n_ctx_chunks - p)
    return jnp.where(d == 0, p, rev_idx)


def _rwkv_kernel(r_ref, v_ref, kk_ref, lw_ref, a_ref, kt_ref, y_ref, s_ref):
    d = pl.program_id(1)

    @pl.when(pl.program_id(2) == 0)
    def _():
        s_ref[...] = jnp.zeros_like(s_ref)

    C = CHUNK
    row, col, strict, incl = _chunk_masks(d)
    lw = lw_ref[0, 0]
    g = _dot(incl[:, :C].astype(F32), lw, HI)
    g_last = jnp.sum(lw, axis=0, keepdims=True)
    eg = jnp.exp(g)
    egx = jnp.exp(g - lw)
    ieg = jnp.exp(-g)
    egl = jnp.exp(g_last)
    kk = kk_ref[0]
    kp_all = kk * egx
    rg_all = r_ref[0] * eg
    kd_all = kt_ref[0, 0] * ieg
    ad_all = kk * a_ref[0, 0] * ieg
    v_all = v_ref[0]
    lane = lax.broadcasted_iota(jnp.int32, (C, 128), 1)
    even = lane < HD
    r128 = lax.broadcasted_iota(jnp.int32, (128, 128), 0)
    c128 = lax.broadcasted_iota(jnp.int32, (128, 128), 1)
    blockdiag = (r128 // HD) == (c128 // HD)
    NP = N_HEADS // 2
    sls = [slice(128 * p, 128 * (p + 1)) for p in range(NP)]
    kp = [kp_all[:, s] for s in sls]
    rg = [rg_all[:, s] for s in sls]
    kd = [kd_all[:, s] for s in sls]
    ad = [ad_all[:, s] for s in sls]
    v = [v_all[:, s] for s in sls]
    el = [egl[:, s] for s in sls]
    gram = [_mm_nt3(jnp.concatenate([jnp.where(even, kp[p], 0.0), jnp.where(even, rg[p], 0.0),
                                     jnp.where(even, 0.0, kp[p]), jnp.where(even, 0.0, rg[p])], axis=0),
                    jnp.concatenate([kd[p], kd[p], ad[p], ad[p]], axis=0)) for p in range(NP)]
    heads = [(p, e) for p in range(NP) for e in range(2)]
    gk = [gram[p][2 * C * e:2 * C * e + C] for p, e in heads]
    gr = [gram[p][2 * C * e + C:2 * C * (e + 1)] for p, e in heads]
    t2 = _unit_tri_inverse([jnp.where(strict, x[:, 2 * C:], 0.0) for x in gk], row, col)
    v_rhs = [_rhs_tok(x) for x in v]
    xy = [_dot(_lhs_tok(jnp.concatenate([jnp.where(strict, gk[i][:, :2 * C], 0.0),
                                         jnp.where(incl, gr[i][:, :2 * C], 0.0)], axis=0)), v_rhs[p])
          for i, (p, e) in enumerate(heads)]
    sol = [_mm_tok(t2[i], jnp.concatenate([kp[p], xy[i][:C]], axis=1))
           for i, (p, e) in enumerate(heads)]
    s0 = [s_ref[p] for p in range(NP)]
    rd = [_mm_nt3(jnp.concatenate([jnp.where(even, sol[2 * p][:, :128], sol[2 * p + 1][:, :128]), rg[p]],
                                  axis=0), s0[p]) for p in range(NP)]
    u = [jnp.where(even, sol[2 * p][:, 128:], sol[2 * p + 1][:, 128:]) + rd[p][:C] for p in range(NP)]
    u_rhs = [_rhs_tok(x) for x in u]
    bu = [_dot(_lhs_tok(jnp.where(incl, gr[i][:, 2 * C:], 0.0)), u_rhs[p]) for i, (p, e) in enumerate(heads)]
    for p in range(NP):
        y_ref[0, 0, :, sls[p]] = (rd[p][C:] + jnp.where(even, xy[2 * p][C:], xy[2 * p + 1][C:])
                                  - jnp.where(even, bu[2 * p], bu[2 * p + 1]))
    upd = [_mm_tn(v[p], kd[p] * el[p]) - _mm_tn(u[p], ad[p] * el[p]) for p in range(NP)]
    for p in range(NP):
        s_ref[p] = s0[p] * el[p] + jnp.where(blockdiag, upd[p], 0.0)


def _rwkv_call(r, v, kk, lw, a, kt):
    B, T, W = r.shape
    nc = T // CHUNK
    ncc = CTX // CHUNK

    def tok(b, d, p):
        return (b, _scan_chunk_index(p, d, ncc, nc), 0)

    def tokd(b, d, p):
        return (d, b, _scan_chunk_index(p, d, ncc, nc), 0)

    return pl.pallas_call(
        _rwkv_kernel,
        grid=(B, 2, nc),
        in_specs=[pl.BlockSpec((1, CHUNK, W), tok)] * 3 + [pl.BlockSpec((1, 1, CHUNK, W), tokd)] * 3,
        out_specs=pl.BlockSpec((1, 1, CHUNK, W), tokd),
        out_shape=jax.ShapeDtypeStruct((2, B, T, W), F32),
        scratch_shapes=[pltpu.VMEM((N_HEADS // 2, 128, 128), F32)],
        compiler_params=_cparams(("parallel", "parallel", "arbitrary")),
        name="rwkv_scan",
    )(r, v, kk, lw, a, kt)


def _gdn_kernel(q_ref, k_ref, v_ref, gb_ref, o_ref, s_ref):
    d = pl.program_id(1)

    @pl.when(pl.program_id(2) == 0)
    def _():
        s_ref[...] = jnp.zeros_like(s_ref)

    C = CHUNK
    row, col, strict, incl = _chunk_masks(d)
    gb = gb_ref[0]
    gc_all = _dot(incl[:, :C].astype(F32), gb, HI)
    incl_t = ((row - col) * (1 - 2 * d) <= 0).astype(F32)
    gct_all = _dot_tn(gb, incl_t, HI)
    gl_all = jnp.sum(gb, axis=0, keepdims=True)
    fwd = d == 0

    def pick(x, lane, axis):
        a = lax.slice_in_dim(x, lane, lane + 1, axis=axis)
        b = lax.slice_in_dim(x, 2 * D_HEADS + lane, 2 * D_HEADS + lane + 1, axis=axis)
        return jnp.where(fwd, a, b)

    H = range(D_HEADS)
    sls = [slice(D_HD * h, D_HD * (h + 1)) for h in H]
    q = [q_ref[0, :, s] for s in sls]
    k = [k_ref[0, :, s] for s in sls]
    v = [v_ref[0, :, s] for s in sls]
    gc = [pick(gc_all, h, 1) for h in H]
    gct = [pick(gct_all, h, 0) for h in H]
    beta = [pick(gb, D_HEADS + h, 1) for h in H]
    gl = [pick(gl_all, h, 1) for h in H]
    dmask = [jnp.where(incl, jnp.exp(jnp.where(incl, gc[h] - gct[h], 0.0)), 0.0) for h in H]
    kb = [k[h] * beta[h] for h in H]
    eg = [jnp.exp(gc[h]) for h in H]
    gram = [_mm_nt3(jnp.concatenate([kb[h], q[h]], axis=0), jnp.concatenate([k[h], k[h]], axis=0)) for h in H]
    t2 = _unit_tri_inverse([jnp.where(strict, gram[h][:C] * dmask[h], 0.0) for h in H], row, col)
    sol = [_mm_tok(t2[h], jnp.concatenate([v[h] * beta[h], kb[h] * eg[h]], axis=1)) for h in H]
    s0 = [s_ref[h] for h in H]
    rd = [_mm_ch3(jnp.concatenate([sol[h][:, D_HD:], q[h] * eg[h]], axis=0), s0[h]) for h in H]
    vnew = [sol[h][:, :D_HD] - rd[h][:C] for h in H]
    qv = [_mm_tok(gram[h][C:] * dmask[h], vnew[h]) for h in H]
    for h in H:
        o_ref[0, 0, :, sls[h]] = rd[h][C:] + qv[h]
    upd = [_mm_tn(k[h] * jnp.exp(gl[h] - gc[h]), vnew[h]) for h in H]
    for h in H:
        s_ref[h] = s0[h] * jnp.exp(gl[h]) + upd[h]


def _gdn_call(q, k, v, gb):
    B, T, W = q.shape
    nc = T // CHUNK
    ncc = CTX // CHUNK

    def tok(b, d, p):
        return (b, _scan_chunk_index(p, d, ncc, nc), 0)

    def tokd(b, d, p):
        return (d, b, _scan_chunk_index(p, d, ncc, nc), 0)

    return pl.pallas_call(
        _gdn_kernel,
        grid=(B, 2, nc),
        in_specs=[pl.BlockSpec((1, CHUNK, W), tok)] * 3 + [pl.BlockSpec((1, CHUNK, 128), tok)],
        out_specs=pl.BlockSpec((1, 1, CHUNK, W), tokd),
        out_shape=jax.ShapeDtypeStruct((2, B, T, W), F32),
        scratch_shapes=[pltpu.VMEM((D_HEADS, D_HD, D_HD), F32)],
        compiler_params=_cparams(("parallel", "parallel", "arbitrary")),
        name="gdn_scan",
    )(q, k, v, gb)


FT = CTX


def _halo_specs(width, lane_block, n_tiles):
    rb = FT // 8
    return [pl.BlockSpec((1, FT, width), lambda b, i: (b, i, lane_block)),
            pl.BlockSpec((1, 8, width), lambda b, i: (b, jnp.maximum(i * rb - 1, 0), lane_block)),
            pl.BlockSpec((1, 8, width), lambda b, i: (b, jnp.minimum((i + 1) * rb, n_tiles * rb - 1), lane_block))]


def _halo_flags():
    i = pl.program_id(1)
    has_prev = (i >= 2).astype(F32)
    has_next = ((i >= 1) & (i < pl.num_programs(1) - 1)).astype(F32)
    return has_prev, has_next


def _shifted(z, k, prev8, next8, row):
    tm = z.shape[0]
    if k > 0:
        out = pltpu.roll(z, k, 0)
        for j in range(k):
            out = jnp.where(row == j, prev8[8 - k + j:8 - k + j + 1], out)
    else:
        out = pltpu.roll(z, tm + k, 0)
        for j in range(-k):
            out = jnp.where(row == tm + k + j, next8[j:j + 1], out)
    return out


def _group_mean(x, hd):
    r = lax.broadcasted_iota(jnp.int32, (128, 128), 0) // hd
    c = lax.broadcasted_iota(jnp.int32, (128, 128), 1) // hd
    return _dot(x, jnp.where(r == c, 1.0 / hd, 0.0).astype(F32), HI)


def _rwkv_feat_kernel(z_ref, zp_ref, zn_ref, l_ref, lp_ref, ln_ref, mu_ref, mul_ref, wl_ref, b0_ref,
                      kk_ref, ka_ref, rk_ref, r_out, v_out, kk_out, lw_out, a_out, kt_out, bon_out):
    W = BW
    has_prev, has_next = _halo_flags()
    row = lax.broadcasted_iota(jnp.int32, (FT, 1), 0)

    def token_shift(x_ref, xp_ref, xn_ref, m_ref):
        z = x_ref[0]
        prev = _shifted(z, 1, xp_ref[0] * has_prev, None, row)
        nxt = _shifted(z, -1, None, xn_ref[0] * has_next, row)
        return z + m_ref[0:1] * (prev - z) + m_ref[1:2] * (nxt - z)

    z = token_shift(z_ref, zp_ref, zn_ref, mu_ref)
    zl = token_shift(l_ref, lp_ref, ln_ref, mul_ref)
    r, k, v = z[:, :W], z[:, W:2 * W], z[:, 2 * W:]
    lane = lax.broadcasted_iota(jnp.int32, zl.shape, 1)
    zl = jnp.where(lane < B_LORA, jnp.tanh(zl), zl)
    r_out[0] = r
    v_out[0] = v
    kk = k * kk_ref[...]
    kks = []
    for c in range(W // 128):
        x = kk[:, 128 * c:128 * (c + 1)]
        kks.append(x * lax.rsqrt(jnp.maximum(_group_mean(x * x, HD) * HD, EPS * EPS)))
    kk_out[0] = jnp.concatenate(kks, axis=1)
    bonus = None
    for d in range(2):
        pre = _dot(zl, wl_ref[d], HI) + b0_ref[d]
        lw_out[d, 0] = -B_DECAY_SCALE * jax.nn.sigmoid(pre[:, :W])
        a = jax.nn.sigmoid(pre[:, W:])
        a_out[d, 0] = a
        kt = k * (1.0 + (a - 1.0) * ka_ref[...])
        kt_out[d, 0] = kt
        rkk = r * kt * rk_ref[...]
        b = jnp.concatenate([_group_mean(rkk[:, 128 * c:128 * (c + 1)], HD) * HD for c in range(W // 128)],
                            axis=1) * v
        bonus = b if bonus is None else bonus + b
    bon_out[0] = bonus


def _rwkv_feat_call(p, mu, w0, wup, a0, aup, k_k, k_a, r_k):
    B, T, _ = p.shape
    W = BW
    nt = T // FT
    zero = jnp.zeros((2, B_LORA, W), F32)
    wl = jnp.concatenate([jnp.concatenate([wup, zero], axis=2), jnp.concatenate([zero, aup], axis=2)], axis=1)
    b0 = jnp.concatenate([w0, a0], axis=1).reshape(2, 1, 2 * W)
    full = lambda shape: pl.BlockSpec(shape, lambda b, i: (0,) * len(shape))
    tok = pl.BlockSpec((1, FT, W), lambda b, i: (b, i, 0))
    tokd = pl.BlockSpec((2, 1, FT, W), lambda b, i: (0, b, i, 0))
    sd = jax.ShapeDtypeStruct
    return pl.pallas_call(
        _rwkv_feat_kernel,
        grid=(B, nt),
        in_specs=_halo_specs(3 * W, P_BRKV // (3 * W), nt) + _halo_specs(128, P_BL // 128, nt)
        + [full((2, 3 * W)), full((2, 128)), full((2, 128, 2 * W)), full((2, 1, 2 * W)),
           full((1, W)), full((1, W)), full((1, W))],
        out_specs=[tok, tok, tok, tokd, tokd, tokd, tok],
        out_shape=[sd((B, T, W), F32)] * 3 + [sd((2, B, T, W), F32)] * 3 + [sd((B, T, W), F32)],
        compiler_params=_cparams(("parallel", "parallel")),
        name="rwkv_feat",
    )(p, p, p, p, p, p, mu[:, :3 * W], mu[:, 3 * W:], wl, b0,
      k_k.reshape(1, W), k_a.reshape(1, W), r_k.reshape(1, W))


def _gdn_feat_kernel(z_ref, zp_ref, zn_ref, ab_ref, cw_ref, lp_ref, q_out, k_out, v_out, gb_out):
    W = BW
    has_prev, has_next = _halo_flags()
    row = lax.broadcasted_iota(jnp.int32, (FT, 1), 0)
    z = z_ref[0]
    prev8 = zp_ref[0] * has_prev
    next8 = zn_ref[0] * has_next
    half = D_CONV // 2
    conv = cw_ref[half:half + 1] * z
    for i in range(D_CONV):
        if i != half:
            conv = conv + cw_ref[i:i + 1] * _shifted(z, half - i, prev8, next8, row)
    qkv = conv * jax.nn.sigmoid(conv)

    def l2n(x):
        return x * lax.rsqrt(jnp.maximum(jnp.sum(x * x, axis=-1, keepdims=True), EPS * EPS))

    for h in range(D_HEADS):
        sl = slice(D_HD * h, D_HD * (h + 1))
        q_out[0, :, sl] = l2n(qkv[:, sl]) * (D_HD ** -0.5)
        k_out[0, :, sl] = l2n(qkv[:, W + D_HD * h:W + D_HD * (h + 1)])
    v_out[0] = qkv[:, 2 * W:]
    x = ab_ref[0]
    y = x + lp_ref[1:2]
    softplus = jnp.maximum(y, 0.0) + jnp.log1p(jnp.exp(-jnp.abs(y)))
    gb_out[0] = jnp.where(lp_ref[2:3] > 0.5, lp_ref[0:1] * softplus, jax.nn.sigmoid(x))


def _gdn_feat_call(p, conv_w, alog, dtb):
    B, T, _ = p.shape
    W, H = BW, D_HEADS
    nt = T // FT
    is_a = jnp.tile(jnp.concatenate([jnp.ones((H,), F32), jnp.zeros((H,), F32)]), 2)
    nea = jnp.concatenate([-jnp.exp(alog[0]), jnp.zeros((H,), F32), -jnp.exp(alog[1]), jnp.zeros((H,), F32)])
    dtl = jnp.concatenate([dtb[0], jnp.zeros((H,), F32), dtb[1], jnp.zeros((H,), F32)])
    lp = jnp.pad(jnp.stack([nea, dtl, is_a]), ((0, 5), (0, 128 - 4 * H)))
    cw = jnp.pad(conv_w, ((0, 8 - D_CONV), (0, 0)))
    full = lambda shape: pl.BlockSpec(shape, lambda b, i: (0,) * len(shape))
    tok = pl.BlockSpec((1, FT, W), lambda b, i: (b, i, 0))
    sd = jax.ShapeDtypeStruct
    return pl.pallas_call(
        _gdn_feat_kernel,
        grid=(B, nt),
        in_specs=_halo_specs(3 * W, P_DQKV // (3 * W), nt)
        + [pl.BlockSpec((1, FT, 128), lambda b, i: (b, i, P_DAB // 128)), full((8, 3 * W)), full((8, 128))],
        out_specs=[tok, tok, tok, pl.BlockSpec((1, FT, 128), lambda b, i: (b, i, 0))],
        out_shape=[sd((B, T, W), F32)] * 3 + [sd((B, T, 128), F32)],
        compiler_params=_cparams(("parallel", "parallel")),
        name="gdn_feat",
    )(p, p, p, p, cw, lp)


def _branch_kernel(pm_ref, ya_ref, yb0_ref, yb1_ref, bon_ref, gb_ref, yc_ref, yd0_ref, yd1_ref, gd_ref,
                   ln_ref, dn_ref, gup_ref, gbias_ref, wbr_ref, o_ref):
    def silu(x):
        return x * jax.nn.sigmoid(x)

    wkv = yb0_ref[0, 0] + yb1_ref[0, 0]
    gn = []
    for c in range(BW // 128):
        x = wkv[:, 128 * c:128 * (c + 1)]
        cen = x - _group_mean(x, HD)
        gn.append(cen * lax.rsqrt(_group_mean(cen * cen, HD) + B_GN_EPS))
    y_b = (jnp.concatenate(gn, axis=1) * ln_ref[0:1] + ln_ref[1:2] + bon_ref[0]) * silu(gb_ref[0])
    o = yd0_ref[0, 0] + yd1_ref[0, 0]
    on = []
    for h in range(D_HEADS):
        x = o[:, D_HD * h:D_HD * (h + 1)]
        on.append(x * lax.rsqrt(jnp.mean(x * x, axis=-1, keepdims=True) + EPS))
    y_d = jnp.concatenate(on, axis=1) * dn_ref[...] * silu(gd_ref[0])

    pm = pm_ref[0].astype(BF16)
    acc = None
    for i, y in enumerate((ya_ref[0], y_b, yc_ref[0], y_d)):
        gate = jax.nn.sigmoid(_dot(pm, gup_ref[i]) + gbias_ref[i])
        term = gate * _dot(y.astype(BF16), wbr_ref[i])
        acc = term if acc is None else acc + term
    o_ref[0] = acc.astype(BF16)


def _branch_call(p, y_a, y_b2, bonus, y_c, y_d2, ln_g, ln_b, d_norm, g_up, g_b, w_br, tm=256):
    B, T, _ = p.shape
    D = D_MODEL
    tok = pl.BlockSpec((1, tm, BW), lambda b, i: (b, i, 0))
    dir0 = pl.BlockSpec((1, 1, tm, BW), lambda b, i: (0, b, i, 0))
    dir1 = pl.BlockSpec((1, 1, tm, BW), lambda b, i: (1, b, i, 0))
    pcol = lambda off: pl.BlockSpec((1, tm, BW), lambda b, i: (b, i, off // BW))
    full = lambda shape: pl.BlockSpec(shape, lambda b, i: (0,) * len(shape))
    return pl.pallas_call(
        _branch_kernel,
        grid=(B, T // tm),
        in_specs=[pl.BlockSpec((1, tm, GATE_RANK), lambda b, i: (b, i, P_G // GATE_RANK)),
                  tok, dir0, dir1, tok, pcol(P_BG), tok, dir0, dir1, pcol(P_DG),
                  full((2, BW)), full((1, BW)),
                  full((N_BRANCH, GATE_RANK, D)), full((N_BRANCH, 1, D)), full((N_BRANCH, BW, D))],
        out_specs=pl.BlockSpec((1, tm, D), lambda b, i: (b, i, 0)),
        out_shape=jax.ShapeDtypeStruct((B, T, D), BF16),
        compiler_params=_cparams(("parallel", "parallel")),
        name="branch_merge",
    )(p, y_a, y_b2, y_b2, bonus, p, y_c, y_d2, y_d2, p, jnp.stack([ln_g, ln_b]),
      jnp.tile(d_norm, D_HEADS).reshape(1, BW), g_up, g_b.reshape(N_BRANCH, 1, D), w_br)


def _outproj_kernel(x_ref, acc_ref, w_ref, mod_ref, fg_ref, o_ref, *, tm, tile_off, final):
    row = (pl.program_id(1) + tile_off) * tm + lax.broadcasted_iota(jnp.int32, (tm, 1), 0)
    m = mod_ref[0]
    gate = jnp.where(row < CTX, m[5:6], m[4:5])
    y = x_ref[0] + gate * _dot(acc_ref[0], w_ref[...])
    if final:
        ms = jnp.mean(y * y, axis=-1, keepdims=True)
        y = y * lax.rsqrt(ms + EPS) * fg_ref[...]
    o_ref[0] = y


def _outproj_call(x, acc, w_out, modv, final_g, final, tm=256):
    B, T, D = x.shape
    tile_off = CTX // tm if final else 0
    t_out = T - CTX if final else T
    tok_in = lambda b, i: (b, i + tile_off, 0)
    return pl.pallas_call(
        functools.partial(_outproj_kernel, tm=tm, tile_off=tile_off, final=final),
        grid=(B, t_out // tm),
        in_specs=[pl.BlockSpec((1, tm, D), tok_in),
                  pl.BlockSpec((1, tm, D), tok_in),
                  pl.BlockSpec((D, D), lambda b, i: (0, 0)),
                  pl.BlockSpec((1, 8, D), lambda b, i: (b, 0, 0)),
                  pl.BlockSpec((1, D), lambda b, i: (0, 0))],
        out_specs=pl.BlockSpec((1, tm, D), lambda b, i: (b, i, 0)),
        out_shape=jax.ShapeDtypeStruct((B, t_out, D), F32),
        compiler_params=_cparams(("parallel", "parallel")),
        name="outproj",
    )(x, acc, w_out, modv, final_g.reshape(1, D))


def _pad_w_in(w):
    oa, ob, oc, od = 0, 1280, 3456, 4736
    og = od + 2064
    D = w.shape[0]
    z = lambda n: jnp.zeros((D, n), w.dtype)
    a_q, a_kv, a_g = w[:, oa:oa + 512], w[:, oa + 512:oa + 768], w[:, oa + 768:oa + 1280]
    b_rkv, b_l, b_g = w[:, ob:ob + 1536], w[:, ob + 1536:ob + 1664], w[:, ob + 1664:ob + 2176]
    c_q, c_kv, c_g = w[:, oc:oc + 512], w[:, oc + 512:oc + 768], w[:, oc + 768:oc + 1280]
    d_qkv, d_ab, d_g = w[:, od:od + 1536], w[:, od + 1536:od + 1552], w[:, od + 1552:od + 2064]
    g = w[:, og:og + GATE_RANK]
    cols = [a_q, a_g, c_q, c_g, b_g, d_g, b_rkv, d_qkv, a_kv, c_kv, g, b_l, d_ab, z(112)]
    return jnp.concatenate(cols, axis=1).astype(BF16)


def _rope_tables(n):
    rows = n // GRID_W
    row = jnp.repeat(jnp.arange(rows, dtype=F32), GRID_W)
    col = jnp.tile(jnp.arange(GRID_W, dtype=F32), rows)
    half = HD // 2
    inv = ROPE_THETA ** (-jnp.arange(0, half, 2, dtype=F32) / half)
    ar = row[:, None] * inv
    ac = col[:, None] * inv
    cos = jnp.concatenate([jnp.cos(ar), jnp.cos(ar), jnp.cos(ac), jnp.cos(ac)], axis=-1)
    sin = jnp.concatenate([-jnp.sin(ar), jnp.sin(ar), -jnp.sin(ac), jnp.sin(ac)], axis=-1)
    cos = jnp.concatenate([jnp.ones((CTX, HD), F32), cos], axis=0)
    sin = jnp.concatenate([jnp.zeros((CTX, HD), F32), sin], axis=0)
    return jnp.tile(cos, (1, 2)), jnp.tile(sin, (1, 2))


def kernel(x, c, ctx, c_ctx, norm_g, w_mod, b_mod, w_in, a_sink, b_mu, b_w0, b_wup, b_a0, b_aup,
           b_kk, b_ka, b_rk, b_lng, b_lnb, c_qn, c_kn, d_conv, d_alog, d_dtb, d_norm, g_up, g_b,
           w_br, w_out, final_g):
    B, n, D = x.shape
    L = w_mod.shape[0]
    T = CTX + n
    xa = jnp.concatenate([ctx, x], axis=1)
    cs = jnp.zeros((8, D), F32).at[:B].set(c).at[B].set(c_ctx)
    mod = _mod_call(cs, w_mod, b_mod)
    cos, sin = _rope_tables(n)
    ones = jnp.ones((1, 128), F32)
    tm_in = 768 if T % 768 == 0 else 256
    for l in range(L):
        shift, scale, gate = jnp.split(mod[l], 3, axis=-1)
        bc = lambda v: jnp.broadcast_to(v[B], (B, D))
        modv = jnp.stack([scale[:B], shift[:B], bc(scale), bc(shift), gate[:B], bc(gate),
                          jnp.zeros((B, D), F32), jnp.zeros((B, D), F32)], axis=1)
        p = _inproj_call(xa, modv, norm_g[l], _pad_w_in(w_in[l]), tm=tm_in)
        qz, k, v = _attn_prep_call(p, cos, sin, ones, ones, P_AQ, P_AKV, False, False)
        y_a = _flash_w_call(a_sink[l], qz, k, v, p, P_AG)
        qz, k, vt = _attn_prep_call(p, cos, sin, jnp.tile(c_qn[l], 2)[None], jnp.tile(c_kn[l], 2)[None],
                                    P_CQ, P_CKV, True, True)
        y_c = _flash_t_call(qz, k, vt, p, P_CG)
        r, vb, kk, lw, a, kt, bonus = _rwkv_feat_call(p, b_mu[l], b_w0[l], b_wup[l], b_a0[l], b_aup[l],
                                                      b_kk[l], b_ka[l], b_rk[l])
        y_b2 = _rwkv_call(r, vb, kk, lw, a, kt)
        qd, kd, vd, gb = _gdn_feat_call(p, d_conv[l], d_alog[l], d_dtb[l])
        y_d2 = _gdn_call(qd, kd, vd, gb)
        acc = _branch_call(p, y_a, y_b2, bonus, y_c, y_d2, b_lng[l], b_lnb[l], d_norm[l],
                           g_up[l].astype(BF16), g_b[l], w_br[l].astype(BF16))
        xa = _outproj_call(xa, acc, w_out[l].astype(BF16), modv, final_g, l == L - 1)
    return xa
```

```python
import functools
import math

import jax
import jax.numpy as jnp
from jax import lax
from jax.experimental import pallas as pl
from jax.experimental.pallas import tpu as pltpu

F32 = jnp.float32
BF16 = jnp.bfloat16
HI = lax.Precision.HIGHEST

D_MODEL = 2048
DEPTH = 4
GRID_W = 64
CTX = 256
N_BRANCH = 4
BW = D_MODEL // 4
HD = 64
N_HEADS = BW // HD
N_KV = 2
KVW = N_KV * HD
WINDOW = 128
B_LORA = 64
B_DECAY_SCALE = 0.606531
B_GN_EPS = 64e-5
D_HD = 128
D_HEADS = BW // D_HD
D_CONV = 5
GATE_RANK = D_MODEL // 8
ROPE_THETA = 10000.0
EPS = 1e-6
NEG = -1e30
CHUNK = 64

P_AQ, P_AG, P_CQ, P_CG, P_BG, P_DG = 0, 512, 1024, 1536, 2048, 2560
P_BRKV, P_DQKV = 3072, 4608
P_AKV, P_CKV, P_G = 6144, 6400, 6656
P_BL, P_DAB = 6912, 7040
P_W = 7168

VMEM_LIMIT = 56 * 1024 * 1024


def _cparams(sem):
    return pltpu.CompilerParams(dimension_semantics=sem, vmem_limit_bytes=VMEM_LIMIT)


def _dot(a, b, prec=None):
    return jnp.dot(a, b, precision=prec, preferred_element_type=F32)


def _dot_nt(a, b, prec=None):
    return lax.dot_general(a, b, (((1,), (1,)), ((), ())), precision=prec,
                           preferred_element_type=F32)


def _dot_tn(a, b, prec=None):
    return lax.dot_general(a, b, (((0,), (0,)), ((), ())), precision=prec,
                           preferred_element_type=F32)


def _mod_kernel(c_ref, w_ref, b_ref, o_ref):
    c = c_ref[...]
    s = c * jax.nn.sigmoid(c)
    o_ref[0] = _dot(s, w_ref[0], HI) + b_ref[0]


def _mod_call(cs, w_mod, b_mod):
    L, D, D3 = w_mod.shape
    tn = 768
    return pl.pallas_call(
        _mod_kernel,
        grid=(L, D3 // tn),
        in_specs=[pl.BlockSpec((8, D), lambda l, j: (0, 0)),
                  pl.BlockSpec((1, D, tn), lambda l, j: (l, 0, j)),
                  pl.BlockSpec((1, 1, tn), lambda l, j: (l, 0, j))],
        out_specs=pl.BlockSpec((1, 8, tn), lambda l, j: (l, 0, j)),
        out_shape=jax.ShapeDtypeStruct((L, 8, D3), F32),
        compiler_params=_cparams(("parallel", "parallel")),
        name="mod",
    )(cs, w_mod, b_mod.reshape(L, 1, D3))


def _inproj_kernel(x_ref, mod_ref, g_ref, w_ref, o_ref, h_ref, *, tm):
    @pl.when(pl.program_id(2) == 0)
    def _():
        x = x_ref[0]
        ms = jnp.mean(x * x, axis=-1, keepdims=True)
        y = x * lax.rsqrt(ms + EPS) * g_ref[...]
        row = pl.program_id(1) * tm + lax.broadcasted_iota(jnp.int32, (tm, 1), 0)
        is_ctx = row < CTX
        m = mod_ref[0]
        scale = jnp.where(is_ctx, m[2:3], m[0:1])
        shift = jnp.where(is_ctx, m[3:4], m[1:2])
        h_ref[...] = (y * (1.0 + scale) + shift).astype(BF16)

    o_ref[0] = _dot(h_ref[...], w_ref[...])


def _inproj_call(x, modv, norm_g, w_pad, tm=768, tn=512):
    B, T, D = x.shape
    return pl.pallas_call(
        functools.partial(_inproj_kernel, tm=tm),
        grid=(B, T // tm, P_W // tn),
        in_specs=[pl.BlockSpec((1, tm, D), lambda b, i, j: (b, i, 0)),
                  pl.BlockSpec((1, 8, D), lambda b, i, j: (b, 0, 0)),
                  pl.BlockSpec((1, D), lambda b, i, j: (0, 0)),
                  pl.BlockSpec((D, tn), lambda b, i, j: (0, j))],
        out_specs=pl.BlockSpec((1, tm, tn), lambda b, i, j: (b, i, j)),
        out_shape=jax.ShapeDtypeStruct((B, T, P_W), F32),
        scratch_shapes=[pltpu.VMEM((tm, D), BF16)],
        compiler_params=_cparams(("parallel", "parallel", "arbitrary")),
        name="inproj",
    )(x, modv, norm_g.reshape(1, D), w_pad)


def _rope128(x, cos, sin_signed):
    lane = lax.broadcasted_iota(jnp.int32, x.shape, 1)
    lo = (lane % 32) < 16
    rot = jnp.where(lo, pltpu.roll(x, 128 - 16, 1), pltpu.roll(x, 16, 1))
    return x * cos + rot * sin_signed


def _head_ms(x):
    r = lax.broadcasted_iota(jnp.int32, (128, 128), 0) // HD
    c = lax.broadcasted_iota(jnp.int32, (128, 128), 1) // HD
    ones = jnp.where(r == c, 1.0 / HD, 0.0).astype(F32)
    return _dot(x * x, ones, HI)


def _attn_prep_kernel(q_ref, kv_ref, cos_ref, sin_ref, qn_ref, kn_ref, qz_ref, k_ref, v_ref,
                      *, use_norm, v_transposed):
    cos = cos_ref[...]
    sin = sin_ref[...]
    tm = cos.shape[0]
    lane = lax.broadcasted_iota(jnp.int32, (tm, 128), 1)
    scale = HD ** -0.5 * LOG2E
    kv = kv_ref[0]
    k = kv[:, :KVW]
    if use_norm:
        k = k * lax.rsqrt(_head_ms(k) + EPS) * kn_ref[...]
    k_ref[0] = _rope128(k, cos, sin).astype(BF16)
    if v_transposed:
        v_ref[0, 0] = kv[:, KVW:].T.astype(BF16)
    else:
        v_ref[0] = kv[:, KVW:].astype(BF16)
    q = q_ref[0]
    for p in range(N_HEADS // 2):
        qp = q[:, 128 * p:128 * (p + 1)]
        if use_norm:
            qp = qp * lax.rsqrt(_head_ms(qp) + EPS) * qn_ref[...]
        qp = _rope128(qp, cos, sin) * scale
        g = (2 * p) // (N_HEADS // N_KV)
        other = pltpu.roll(qp, 64, 1)
        if g == 0:
            h0 = jnp.where(lane < 64, qp, 0.0)
            h1 = jnp.where(lane < 64, other, 0.0)
        else:
            h0 = jnp.where(lane >= 64, other, 0.0)
            h1 = jnp.where(lane >= 64, qp, 0.0)
        qz_ref[0, :, 256 * p:256 * p + 128] = h0.astype(BF16)
        qz_ref[0, :, 256 * p + 128:256 * p + 256] = h1.astype(BF16)


def _attn_prep_call(p, cos, sin, qn, kn, q_off, kv_off, use_norm, v_transposed):
    B, T, _ = p.shape
    tm = KC
    if v_transposed:
        v_spec = pl.BlockSpec((1, 1, KVW, tm), lambda b, i: (b, i, 0, 0))
        v_shape = jax.ShapeDtypeStruct((B, T // tm, KVW, tm), BF16)
    else:
        v_spec = pl.BlockSpec((1, tm, KVW), lambda b, i: (b, i, 0))
        v_shape = jax.ShapeDtypeStruct((B, T, KVW), BF16)
    return pl.pallas_call(
        functools.partial(_attn_prep_kernel, use_norm=use_norm, v_transposed=v_transposed),
        grid=(B, T // tm),
        in_specs=[pl.BlockSpec((1, tm, BW), lambda b, i: (b, i, q_off // BW)),
                  pl.BlockSpec((1, tm, 2 * KVW), lambda b, i: (b, i, kv_off // (2 * KVW))),
                  pl.BlockSpec((tm, 128), lambda b, i: (i, 0)),
                  pl.BlockSpec((tm, 128), lambda b, i: (i, 0)),
                  pl.BlockSpec((1, 128), lambda b, i: (0, 0)),
                  pl.BlockSpec((1, 128), lambda b, i: (0, 0))],
        out_specs=[pl.BlockSpec((1, tm, N_HEADS * 128), lambda b, i: (b, i, 0)),
                   pl.BlockSpec((1, tm, KVW), lambda b, i: (b, i, 0)),
                   v_spec],
        out_shape=[jax.ShapeDtypeStruct((B, T, N_HEADS * 128), BF16),
                   jax.ShapeDtypeStruct((B, T, KVW), BF16),
                   v_shape],
        compiler_params=_cparams(("parallel", "parallel")),
        name="attn_prep",
    )(p, p, cos, sin, qn, kn)


LOG2E = 1.4426950408889634
KC = 256


def _flash_w_kernel(sink_ref, qz_ref, k_ref, v_ref, g_ref, o_ref, m_ref, l_ref, acc_ref, *, tq):
    qi = pl.program_id(1)
    T = k_ref.shape[1]

    for h in range(N_HEADS):
        m_ref[h] = jnp.full((tq, 128), sink_ref[h] * LOG2E, F32)
        l_ref[h] = jnp.ones((tq, 128), F32)
    acc_ref[...] = jnp.zeros_like(acc_ref)

    def update(starts, masks):
        H = range(N_HEADS)
        ks = [k_ref[0, pl.ds(s, KC), :] for s in starts]
        vs = [v_ref[0, pl.ds(s, KC), :] for s in starts]
        s = [[_dot_nt(qz_ref[0, :, 128 * h:128 * (h + 1)], kc) for kc in ks] for h in H]
        m_new, alpha = [], []
        for h in H:
            mx = None
            for i, mk in enumerate(masks):
                if mk is not None:
                    s[h][i] = jnp.where(mk, s[h][i], NEG)
                mx = s[h][i] if mx is None else jnp.maximum(mx, s[h][i])
            m_prev = m_ref[h]
            m_new.append(jnp.maximum(m_prev, jnp.max(mx, axis=-1, keepdims=True)))
            alpha.append(jnp.exp2(m_prev - m_new[h]))
        for h in H:
            m_wide = jnp.concatenate([m_new[h]] * (KC // 128), axis=1)
            lsum = None
            pv = None
            for sc, vc in zip(s[h], vs):
                pc = jnp.exp2(sc - m_wide)
                lsum = pc if lsum is None else lsum + pc
                d = _dot(pc.astype(BF16), vc)
                pv = d if pv is None else pv + d
            m_ref[h] = m_new[h]
            l_ref[h] = alpha[h] * l_ref[h] + jnp.sum(lsum, axis=-1, keepdims=True)
            acc_ref[h] = alpha[h] * acc_ref[h] + pv

    ctx_starts = [KC * c for c in range(CTX // KC)]

    @pl.when(qi == 0)
    def _():
        update(ctx_starts, [None] * len(ctx_starts))

    @pl.when(qi > 0)
    def _():
        row = lax.broadcasted_iota(jnp.int32, (tq, KC), 0)
        col = lax.broadcasted_iota(jnp.int32, (tq, KC), 1)
        t0 = qi * tq
        starts, masks = [], []
        for c in range((tq + 2 * WINDOW) // KC):
            k0 = t0 - WINDOW + KC * c
            k0c = jnp.minimum(k0, T - KC)
            kpos = k0c + col
            dist = t0 + row - kpos
            masks.append((jnp.abs(dist) <= WINDOW) & (kpos >= jnp.maximum(k0, CTX)))
            starts.append(pl.multiple_of(k0c, WINDOW))
        update(ctx_starts + starts, [None] * len(ctx_starts) + masks)

    lane = lax.broadcasted_iota(jnp.int32, (tq, 128), 1)
    for p in range(N_HEADS // 2):
        g = (2 * p) // (N_HEADS // N_KV)
        a0 = acc_ref[2 * p] / l_ref[2 * p]
        a1 = acc_ref[2 * p + 1] / l_ref[2 * p + 1]
        if g == 0:
            o = jnp.where(lane < 64, a0, pltpu.roll(a1, 64, 1))
        else:
            o = jnp.where(lane < 64, pltpu.roll(a0, 64, 1), a1)
        gate = g_ref[0, :, 128 * p:128 * (p + 1)]
        o_ref[0, :, 128 * p:128 * (p + 1)] = o * (gate * jax.nn.sigmoid(gate))


def _flash_w_call(sink, qz, k, v, p, g_off):
    B, T, _ = qz.shape
    tq = CTX
    return pl.pallas_call(
        functools.partial(_flash_w_kernel, tq=tq),
        grid=(B, T // tq),
        in_specs=[pl.BlockSpec(memory_space=pltpu.SMEM),
                  pl.BlockSpec((1, tq, N_HEADS * 128), lambda b, i: (b, i, 0)),
                  pl.BlockSpec((1, T, KVW), lambda b, i: (b, 0, 0)),
                  pl.BlockSpec((1, T, KVW), lambda b, i: (b, 0, 0)),
                  pl.BlockSpec((1, tq, BW), lambda b, i: (b, i, g_off // BW))],
        out_specs=pl.BlockSpec((1, tq, BW), lambda b, i: (b, i, 0)),
        out_shape=jax.ShapeDtypeStruct((B, T, BW), F32),
        scratch_shapes=[pltpu.VMEM((N_HEADS, tq, 128), F32),
                        pltpu.VMEM((N_HEADS, tq, 128), F32),
                        pltpu.VMEM((N_HEADS, tq, 128), F32)],
        compiler_params=_cparams(("parallel", "parallel")),
        name="flash_window",
    )(sink, qz, k, v, p)


def _flash_t_kernel(qz_ref, k_ref, vt_ref, g_ref, o_ref, m_ref, l_ref, acc_ref, *, tq, tk):
    qi = pl.program_id(1)
    T = k_ref.shape[1]
    per = N_HEADS // N_KV

    m_ref[...] = jnp.full(m_ref.shape, NEG, F32)
    l_ref[...] = jnp.zeros(l_ref.shape, F32)
    acc_ref[...] = jnp.zeros_like(acc_ref)

    def update(chunks):
        H = range(N_HEADS)
        ks = [k_ref[0, pl.ds(pl.multiple_of(c * KC, KC), KC), :] for c in chunks]
        s = [[_dot_nt(kc, qz_ref[0, :, 128 * h:128 * (h + 1)]) for kc in ks] for h in H]
        m_new, alpha = [], []
        for h in H:
            mx = s[h][0]
            for sc in s[h][1:]:
                mx = jnp.maximum(mx, sc)
            m_prev = m_ref[h, 0:1]
            m_new.append(jnp.maximum(m_prev, jnp.max(mx, axis=0, keepdims=True)))
            alpha.append(jnp.exp2(m_prev - m_new[h]))
        for h in H:
            g = h // per
            lsum = None
            pv = None
            for sc, c in zip(s[h], chunks):
                pc = jnp.exp2(sc - m_new[h])
                lsum = pc if lsum is None else lsum + pc
                d = _dot(vt_ref[0, c, HD * g:HD * (g + 1), :], pc.astype(BF16))
                pv = d if pv is None else pv + d
            m_ref[h] = jnp.broadcast_to(m_new[h], (8, tq))
            l_ref[h] = jnp.broadcast_to(alpha[h] * l_ref[h, 0:1] + jnp.sum(lsum, axis=0, keepdims=True), (8, tq))
            acc_ref[h] = alpha[h] * acc_ref[h] + pv

    @pl.when(qi == 0)
    def _():
        update([c for c in range(CTX // KC)])

    @pl.when(qi > 0)
    def _():
        def body(t, carry):
            update([t * (tk // KC) + c for c in range(tk // KC)])
            return carry
        lax.fori_loop(0, T // tk, body, 0)

    o_t = jnp.concatenate([acc_ref[h] / l_ref[h, 0:1] for h in range(N_HEADS)], axis=0)
    gate = g_ref[0]
    o_ref[0] = o_t.T * (gate * jax.nn.sigmoid(gate))


def _flash_t_call(qz, k, vt, p, g_off, tk=768):
    B, T, _ = qz.shape
    tq = CTX
    if T % tk:
        tk = KC
    return pl.pallas_call(
        functools.partial(_flash_t_kernel, tq=tq, tk=tk),
        grid=(B, T // tq),
        in_specs=[pl.BlockSpec((1, tq, N_HEADS * 128), lambda b, i: (b, i, 0)),
                  pl.BlockSpec((1, T, KVW), lambda b, i: (b, 0, 0)),
                  pl.BlockSpec((1, T // KC, KVW, KC), lambda b, i: (b, 0, 0, 0)),
                  pl.BlockSpec((1, tq, BW), lambda b, i: (b, i, g_off // BW))],
        out_specs=pl.BlockSpec((1, tq, BW), lambda b, i: (b, i, 0)),
        out_shape=jax.ShapeDtypeStruct((B, T, BW), F32),
        scratch_shapes=[pltpu.VMEM((N_HEADS, 8, tq), F32),
                        pltpu.VMEM((N_HEADS, 8, tq), F32),
                        pltpu.VMEM((N_HEADS, HD, tq), F32)],
        compiler_params=_cparams(("parallel", "parallel")),
        name="flash_global",
    )(qz, k, vt, p)


def _chunk_masks(d):
    row = lax.broadcasted_iota(jnp.int32, (CHUNK, 2 * CHUNK), 0)
    col = lax.broadcasted_iota(jnp.int32, (CHUNK, 2 * CHUNK), 1) % CHUNK
    lag = (row - col) * (1 - 2 * d)
    return row, col, lag > 0, lag >= 0


def _hl(x):
    hi = x.astype(BF16)
    return hi, (x - hi.astype(F32)).astype(BF16)


def _lhs_tok(m2):
    return jnp.concatenate(_hl(m2), axis=1)


def _rhs_tok(x):
    xh, xl = _hl(x)
    return jnp.concatenate([xh, xl, xh, xl], axis=0)


def _mm_tok(m2, x):
    return _dot(_lhs_tok(m2), _rhs_tok(x))


def _mm_nt3(a, b):
    ah, al = _hl(a)
    bh, bl = _hl(b)
    return (_dot_nt(jnp.concatenate([ah, al], axis=1), jnp.concatenate([bh, bh], axis=1))
            + _dot_nt(ah, bl))


def _mm_ch3(a, s):
    ah, al = _hl(a)
    sh, sl = _hl(s)
    return (_dot(jnp.concatenate([ah, al], axis=1), jnp.concatenate([sh, sh], axis=0))
            + _dot(ah, sl))


def _mm_tn(a, b):
    ah, al = _hl(a)
    bh, bl = _hl(b)
    at = jnp.concatenate([ah.astype(F32), al.astype(F32)], axis=0).T.astype(BF16)
    return _dot(jnp.concatenate([at, at], axis=1), jnp.concatenate([bh, bh, bl, bl], axis=0))


def _unit_tri_inverse(a2, row, col):
    n = len(a2)
    eye = (row == col).astype(F32)
    t2 = [eye - jnp.where((row ^ col) == 1, a, 0.0) for a in a2]
    s = 2
    while s < CHUNK:
        off = ((row ^ col) // s) == 1
        hl = [_hl(t) for t in t2]
        y2 = [_dot(_lhs_tok(jnp.where(off, a2[i], 0.0)),
                   jnp.concatenate([hl[i][0], hl[i][1], hl[i][0], hl[i][1]], axis=0)) for i in range(n)]
        z2 = [_dot(jnp.concatenate(hl[i], axis=1), _rhs_tok(y2[i])) for i in range(n)]
        t2 = [t2[i] - z2[i] for i in range(n)]
        s *= 2
    return t2


def _scan_chunk_index(p, d, n_ctx_chunks, n_chunks):
    rev_idx = jnp.where(p < n_ctx_chunks, n_ctx_chunks - 1 - p, n_chunks - 1 + n_ctx_chunks - p)
    return jnp.where(d == 0, p, rev_idx)


def _cumsum_tok(incl, x):
    x1 = x.astype(BF16)
    r1 = x - x1.astype(F32)
    x2 = r1.astype(BF16)
    x3 = (r1 - x2.astype(F32)).astype(BF16)
    m = incl.astype(BF16)
    return _dot(jnp.concatenate([m, m, m], axis=1), jnp.concatenate([x1, x2, x3], axis=0))


def _rwkv_kernel(rf_ref, vf_ref, kkf_ref, rr_ref, vr_ref, kkr_ref, lw_ref, a_ref, kt_ref,
                 lwr_ref, ar_ref, ktr_ref, yf_ref, yr_ref, s_ref):
    @pl.when(pl.program_id(1) == 0)
    def _():
        s_ref[...] = jnp.zeros_like(s_ref)

    C = CHUNK
    NP = N_HEADS // 2
    lane = lax.broadcasted_iota(jnp.int32, (C, 128), 1)
    even = lane < HD
    r128 = lax.broadcasted_iota(jnp.int32, (128, 128), 0)
    c128 = lax.broadcasted_iota(jnp.int32, (128, 128), 1)
    blockdiag = (r128 // HD) == (c128 // HD)
    sls = [slice(128 * p, 128 * (p + 1)) for p in range(NP)]

    kp, rg, kd, ad, v, el, strict, incl = [], [], [], [], [], [], [], []
    row = col = None
    for d, (r_ref, v_ref, kk_ref, w_ref, aa_ref, k_ref) in enumerate(
            ((rf_ref, vf_ref, kkf_ref, lw_ref, a_ref, kt_ref), (rr_ref, vr_ref, kkr_ref, lwr_ref, ar_ref, ktr_ref))):
        row, col, strict_d, incl_d = _chunk_masks(d)
        lw = w_ref[0, 0]
        g = _cumsum_tok(incl_d[:, :C], lw)
        egl = jnp.exp(jnp.sum(lw, axis=0, keepdims=True))
        kk = kk_ref[0]
        ieg = jnp.exp(-g)
        kp_all = kk * jnp.exp(g - lw)
        rg_all = r_ref[0] * jnp.exp(g)
        kd_all = k_ref[0, 0] * ieg
        ad_all = kk * aa_ref[0, 0] * ieg
        v_all = v_ref[0]
        for sl in sls:
            kp.append(kp_all[:, sl])
            rg.append(rg_all[:, sl])
            kd.append(kd_all[:, sl])
            ad.append(ad_all[:, sl])
            v.append(v_all[:, sl])
            el.append(egl[:, sl])
            strict.append(strict_d)
            incl.append(incl_d)
    U = range(2 * NP)
    gram = [_mm_nt3(jnp.concatenate([jnp.where(even, kp[u], 0.0), jnp.where(even, rg[u], 0.0),
                                     jnp.where(even, 0.0, kp[u]), jnp.where(even, 0.0, rg[u])], axis=0),
                    jnp.concatenate([kd[u], kd[u], ad[u], ad[u]], axis=0)) for u in U]
    heads = [(u, e) for u in U for e in range(2)]
    gk = [gram[u][2 * C * e:2 * C * e + C] for u, e in heads]
    gr = [gram[u][2 * C * e + C:2 * C * (e + 1)] for u, e in heads]
    t2 = _unit_tri_inverse([jnp.where(strict[u], gk[i][:, 2 * C:], 0.0) for i, (u, e) in enumerate(heads)],
                           row, col)
    v_rhs = [_rhs_tok(x) for x in v]
    xy = [_dot(_lhs_tok(jnp.concatenate([jnp.where(strict[u], gk[i][:, :2 * C], 0.0),
                                         jnp.where(incl[u], gr[i][:, :2 * C], 0.0)], axis=0)), v_rhs[u])
          for i, (u, e) in enumerate(heads)]
    sol = [_mm_tok(t2[i], jnp.concatenate([kp[u], xy[i][:C]], axis=1))
           for i, (u, e) in enumerate(heads)]
    s0 = [s_ref[u] for u in U]
    rd = [_mm_nt3(jnp.concatenate([jnp.where(even, sol[2 * u][:, :128], sol[2 * u + 1][:, :128]), rg[u]],
                                  axis=0), s0[u]) for u in U]
    uu = [jnp.where(even, sol[2 * u][:, 128:], sol[2 * u + 1][:, 128:]) + rd[u][:C] for u in U]
    u_rhs = [_rhs_tok(x) for x in uu]
    bu = [_dot(_lhs_tok(jnp.where(incl[u], gr[i][:, 2 * C:], 0.0)), u_rhs[u]) for i, (u, e) in enumerate(heads)]
    for u in U:
        y_ref = yf_ref if u < NP else yr_ref
        y_ref[0, :, sls[u % NP]] = (rd[u][C:] + jnp.where(even, xy[2 * u][C:], xy[2 * u + 1][C:])
                                    - jnp.where(even, bu[2 * u], bu[2 * u + 1]))
    upd = [_mm_tn(v[u], kd[u] * el[u]) - _mm_tn(uu[u], ad[u] * el[u]) for u in U]
    for u in U:
        s_ref[u] = s0[u] * el[u] + jnp.where(blockdiag, upd[u], 0.0)


def _scan_specs(nc, width):
    ncc = CTX // CHUNK
    fwd = pl.BlockSpec((1, CHUNK, width), lambda b, p: (b, p, 0))
    rev = pl.BlockSpec((1, CHUNK, width), lambda b, p: (b, _scan_chunk_index(p, 1, ncc, nc), 0))
    fwd_d = pl.BlockSpec((1, 1, CHUNK, width), lambda b, p: (0, b, p, 0))
    rev_d = pl.BlockSpec((1, 1, CHUNK, width), lambda b, p: (1, b, _scan_chunk_index(p, 1, ncc, nc), 0))
    return fwd, rev, fwd_d, rev_d


def _rwkv_call(r, v, kk, lw, a, kt):
    B, T, W = r.shape
    nc = T // CHUNK
    fwd, rev, fwd_d, rev_d = _scan_specs(nc, W)
    return pl.pallas_call(
        _rwkv_kernel,
        grid=(B, nc),
        in_specs=[fwd] * 3 + [rev] * 3 + [fwd_d] * 3 + [rev_d] * 3,
        out_specs=[fwd, rev],
        out_shape=[jax.ShapeDtypeStruct((B, T, W), F32)] * 2,
        scratch_shapes=[pltpu.VMEM((N_HEADS, 128, 128), F32)],
        compiler_params=_cparams(("parallel", "arbitrary")),
        name="rwkv_scan",
    )(r, v, kk, r, v, kk, lw, a, kt, lw, a, kt)


def _gdn_kernel(qf_ref, kf_ref, vf_ref, gbf_ref, qr_ref, kr_ref, vr_ref, gbr_ref, of_ref, or_ref, s_ref):
    @pl.when(pl.program_id(1) == 0)
    def _():
        s_ref[...] = jnp.zeros_like(s_ref)

    C = CHUNK
    sls = [slice(D_HD * h, D_HD * (h + 1)) for h in range(D_HEADS)]
    q, k, v, gc, gl, beta, dmask, strict = [], [], [], [], [], [], [], []
    row = col = None
    for d, (q_ref, k_ref, v_ref, gb_ref) in enumerate(((qf_ref, kf_ref, vf_ref, gbf_ref),
                                                        (qr_ref, kr_ref, vr_ref, gbr_ref))):
        row, col, strict_d, incl_d = _chunk_masks(d)
        gb = gb_ref[0]
        gc_all = _cumsum_tok(incl_d[:, :C], gb)
        incl_t = ((row - col) * (1 - 2 * d) <= 0).astype(F32)
        gct_all = _dot_tn(gb, incl_t, HI)
        gl_all = jnp.sum(gb, axis=0, keepdims=True)
        for h in range(D_HEADS):
            ln = 2 * D_HEADS * d + h
            q.append(q_ref[0, :, sls[h]])
            k.append(k_ref[0, :, sls[h]])
            v.append(v_ref[0, :, sls[h]])
            gc.append(gc_all[:, ln:ln + 1])
            gl.append(gl_all[:, ln:ln + 1])
            beta.append(gb[:, ln + D_HEADS:ln + D_HEADS + 1])
            dmask.append(jnp.where(incl_d, jnp.exp(jnp.where(incl_d, gc[-1] - gct_all[ln:ln + 1, :], 0.0)), 0.0))
            strict.append(strict_d)
    U = range(2 * D_HEADS)
    kb = [k[u] * beta[u] for u in U]
    eg = [jnp.exp(gc[u]) for u in U]
    gram = [_mm_nt3(jnp.concatenate([kb[u], q[u]], axis=0), jnp.concatenate([k[u], k[u]], axis=0)) for u in U]
    t2 = _unit_tri_inverse([jnp.where(strict[u], gram[u][:C] * dmask[u], 0.0) for u in U], row, col)
    sol = [_mm_tok(t2[u], jnp.concatenate([v[u] * beta[u], kb[u] * eg[u]], axis=1)) for u in U]
    s0 = [s_ref[u] for u in U]
    rd = [_mm_ch3(jnp.concatenate([sol[u][:, D_HD:], q[u] * eg[u]], axis=0), s0[u]) for u in U]
    vnew = [sol[u][:, :D_HD] - rd[u][:C] for u in U]
    qv = [_mm_tok(gram[u][C:] * dmask[u], vnew[u]) for u in U]
    for u in U:
        o_ref = of_ref if u < D_HEADS else or_ref
        o_ref[0, :, sls[u % D_HEADS]] = rd[u][C:] + qv[u]
    upd = [_mm_tn(k[u] * jnp.exp(gl[u] - gc[u]), vnew[u]) for u in U]
    for u in U:
        s_ref[u] = s0[u] * jnp.exp(gl[u]) + upd[u]


def _gdn_call(q, k, v, gb):
    B, T, W = q.shape
    nc = T // CHUNK
    fwd, rev, _, _ = _scan_specs(nc, W)
    fwd_g, rev_g, _, _ = _scan_specs(nc, 128)
    return pl.pallas_call(
        _gdn_kernel,
        grid=(B, nc),
        in_specs=[fwd] * 3 + [fwd_g] + [rev] * 3 + [rev_g],
        out_specs=[fwd, rev],
        out_shape=[jax.ShapeDtypeStruct((B, T, W), F32)] * 2,
        scratch_shapes=[pltpu.VMEM((2 * D_HEADS, D_HD, D_HD), F32)],
        compiler_params=_cparams(("parallel", "arbitrary")),
        name="gdn_scan",
    )(q, k, v, gb, q, k, v, gb)


FT = CTX


def _halo_specs(width, lane_block, n_tiles):
    rb = FT // 8
    return [pl.BlockSpec((1, FT, width), lambda b, i: (b, i, lane_block)),
            pl.BlockSpec((1, 8, width), lambda b, i: (b, jnp.maximum(i * rb - 1, 0), lane_block)),
            pl.BlockSpec((1, 8, width), lambda b, i: (b, jnp.minimum((i + 1) * rb, n_tiles * rb - 1), lane_block))]


def _halo_flags():
    i = pl.program_id(1)
    has_prev = (i >= 2).astype(F32)
    has_next = ((i >= 1) & (i < pl.num_programs(1) - 1)).astype(F32)
    return has_prev, has_next


def _shifted(z, k, prev8, next8, row):
    tm = z.shape[0]
    if k > 0:
        out = pltpu.roll(z, k, 0)
        for j in range(k):
            out = jnp.where(row == j, prev8[8 - k + j:8 - k + j + 1], out)
    else:
        out = pltpu.roll(z, tm + k, 0)
        for j in range(-k):
            out = jnp.where(row == tm + k + j, next8[j:j + 1], out)
    return out


def _group_mean(x, hd):
    r = lax.broadcasted_iota(jnp.int32, (128, 128), 0) // hd
    c = lax.broadcasted_iota(jnp.int32, (128, 128), 1) // hd
    return _dot(x, jnp.where(r == c, 1.0 / hd, 0.0).astype(F32), HI)


def _rwkv_feat_kernel(z_ref, zp_ref, zn_ref, l_ref, lp_ref, ln_ref, mu_ref, mul_ref, wl_ref, b0_ref,
                      kk_ref, ka_ref, rk_ref, r_out, v_out, kk_out, lw_out, a_out, kt_out, bon_out):
    W = BW
    has_prev, has_next = _halo_flags()
    row = lax.broadcasted_iota(jnp.int32, (FT, 1), 0)

    def token_shift(x_ref, xp_ref, xn_ref, m_ref):
        z = x_ref[0]
        prev = _shifted(z, 1, xp_ref[0] * has_prev, None, row)
        nxt = _shifted(z, -1, None, xn_ref[0] * has_next, row)
        return z + m_ref[0:1] * (prev - z) + m_ref[1:2] * (nxt - z)

    z = token_shift(z_ref, zp_ref, zn_ref, mu_ref)
    zl = token_shift(l_ref, lp_ref, ln_ref, mul_ref)
    r, k, v = z[:, :W], z[:, W:2 * W], z[:, 2 * W:]
    lane = lax.broadcasted_iota(jnp.int32, zl.shape, 1)
    zl = jnp.where(lane < B_LORA, jnp.tanh(zl), zl)
    r_out[0] = r
    v_out[0] = v
    kk = k * kk_ref[...]
    kks = []
    for c in range(W // 128):
        x = kk[:, 128 * c:128 * (c + 1)]
        kks.append(x * lax.rsqrt(jnp.maximum(_group_mean(x * x, HD) * HD, EPS * EPS)))
    kk_out[0] = jnp.concatenate(kks, axis=1)
    bonus = None
    for d in range(2):
        pre = _dot(zl, wl_ref[d], HI) + b0_ref[d]
        lw_out[d, 0] = -B_DECAY_SCALE * jax.nn.sigmoid(pre[:, :W])
        a = jax.nn.sigmoid(pre[:, W:])
        a_out[d, 0] = a
        kt = k * (1.0 + (a - 1.0) * ka_ref[...])
        kt_out[d, 0] = kt
        rkk = r * kt * rk_ref[...]
        b = jnp.concatenate([_group_mean(rkk[:, 128 * c:128 * (c + 1)], HD) * HD for c in range(W // 128)],
                            axis=1) * v
        bonus = b if bonus is None else bonus + b
    bon_out[0] = bonus


def _rwkv_feat_call(p, mu, w0, wup, a0, aup, k_k, k_a, r_k):
    B, T, _ = p.shape
    W = BW
    nt = T // FT
    zero = jnp.zeros((2, B_LORA, W), F32)
    wl = jnp.concatenate([jnp.concatenate([wup, zero], axis=2), jnp.concatenate([zero, aup], axis=2)], axis=1)
    b0 = jnp.concatenate([w0, a0], axis=1).reshape(2, 1, 2 * W)
    full = lambda shape: pl.BlockSpec(shape, lambda b, i: (0,) * len(shape))
    tok = pl.BlockSpec((1, FT, W), lambda b, i: (b, i, 0))
    tokd = pl.BlockSpec((2, 1, FT, W), lambda b, i: (0, b, i, 0))
    sd = jax.ShapeDtypeStruct
    return pl.pallas_call(
        _rwkv_feat_kernel,
        grid=(B, nt),
        in_specs=_halo_specs(3 * W, P_BRKV // (3 * W), nt) + _halo_specs(128, P_BL // 128, nt)
        + [full((2, 3 * W)), full((2, 128)), full((2, 128, 2 * W)), full((2, 1, 2 * W)),
           full((1, W)), full((1, W)), full((1, W))],
        out_specs=[tok, tok, tok, tokd, tokd, tokd, tok],
        out_shape=[sd((B, T, W), F32)] * 3 + [sd((2, B, T, W), F32)] * 3 + [sd((B, T, W), F32)],
        compiler_params=_cparams(("parallel", "parallel")),
        name="rwkv_feat",
    )(p, p, p, p, p, p, mu[:, :3 * W], mu[:, 3 * W:], wl, b0,
      k_k.reshape(1, W), k_a.reshape(1, W), r_k.reshape(1, W))


def _gdn_feat_kernel(z_ref, zp_ref, zn_ref, ab_ref, cw_ref, lp_ref, q_out, k_out, v_out, gb_out):
    W = BW
    has_prev, has_next = _halo_flags()
    row = lax.broadcasted_iota(jnp.int32, (FT, 1), 0)
    z = z_ref[0]
    prev8 = zp_ref[0] * has_prev
    next8 = zn_ref[0] * has_next
    half = D_CONV // 2
    conv = cw_ref[half:half + 1] * z
    for i in range(D_CONV):
        if i != half:
            conv = conv + cw_ref[i:i + 1] * _shifted(z, half - i, prev8, next8, row)
    qkv = conv * jax.nn.sigmoid(conv)

    def l2n(x):
        return x * lax.rsqrt(jnp.maximum(jnp.sum(x * x, axis=-1, keepdims=True), EPS * EPS))

    for h in range(D_HEADS):
        sl = slice(D_HD * h, D_HD * (h + 1))
        q_out[0, :, sl] = l2n(qkv[:, sl]) * (D_HD ** -0.5)
        k_out[0, :, sl] = l2n(qkv[:, W + D_HD * h:W + D_HD * (h + 1)])
    v_out[0] = qkv[:, 2 * W:]
    x = ab_ref[0]
    y = x + lp_ref[1:2]
    softplus = jnp.maximum(y, 0.0) + jnp.log1p(jnp.exp(-jnp.abs(y)))
    gb_out[0] = jnp.where(lp_ref[2:3] > 0.5, lp_ref[0:1] * softplus, jax.nn.sigmoid(x))


def _gdn_feat_call(p, conv_w, alog, dtb):
    B, T, _ = p.shape
    W, H = BW, D_HEADS
    nt = T // FT
    is_a = jnp.tile(jnp.concatenate([jnp.ones((H,), F32), jnp.zeros((H,), F32)]), 2)
    nea = jnp.concatenate([-jnp.exp(alog[0]), jnp.zeros((H,), F32), -jnp.exp(alog[1]), jnp.zeros((H,), F32)])
    dtl = jnp.concatenate([dtb[0], jnp.zeros((H,), F32), dtb[1], jnp.zeros((H,), F32)])
    lp = jnp.pad(jnp.stack([nea, dtl, is_a]), ((0, 5), (0, 128 - 4 * H)))
    cw = jnp.pad(conv_w, ((0, 8 - D_CONV), (0, 0)))
    full = lambda shape: pl.BlockSpec(shape, lambda b, i: (0,) * len(shape))
    tok = pl.BlockSpec((1, FT, W), lambda b, i: (b, i, 0))
    sd = jax.ShapeDtypeStruct
    return pl.pallas_call(
        _gdn_feat_kernel,
        grid=(B, nt),
        in_specs=_halo_specs(3 * W, P_DQKV // (3 * W), nt)
        + [pl.BlockSpec((1, FT, 128), lambda b, i: (b, i, P_DAB // 128)), full((8, 3 * W)), full((8, 128))],
        out_specs=[tok, tok, tok, pl.BlockSpec((1, FT, 128), lambda b, i: (b, i, 0))],
        out_shape=[sd((B, T, W), F32)] * 3 + [sd((B, T, 128), F32)],
        compiler_params=_cparams(("parallel", "parallel")),
        name="gdn_feat",
    )(p, p, p, p, cw, lp)


def _branch_kernel(pm_ref, ya_ref, yb0_ref, yb1_ref, bon_ref, gb_ref, yc_ref, yd0_ref, yd1_ref, gd_ref,
                   ln_ref, dn_ref, gup_ref, gbias_ref, wbr_ref, o_ref):
    def silu(x):
        return x * jax.nn.sigmoid(x)

    wkv = yb0_ref[0] + yb1_ref[0]
    gn = []
    for c in range(BW // 128):
        x = wkv[:, 128 * c:128 * (c + 1)]
        cen = x - _group_mean(x, HD)
        gn.append(cen * lax.rsqrt(_group_mean(cen * cen, HD) + B_GN_EPS))
    y_b = (jnp.concatenate(gn, axis=1) * ln_ref[0:1] + ln_ref[1:2] + bon_ref[0]) * silu(gb_ref[0])
    o = yd0_ref[0] + yd1_ref[0]
    on = []
    for h in range(D_HEADS):
        x = o[:, D_HD * h:D_HD * (h + 1)]
        on.append(x * lax.rsqrt(jnp.mean(x * x, axis=-1, keepdims=True) + EPS))
    y_d = jnp.concatenate(on, axis=1) * dn_ref[...] * silu(gd_ref[0])

    pm = pm_ref[0].astype(BF16)
    acc = None
    for i, y in enumerate((ya_ref[0], y_b, yc_ref[0], y_d)):
        gate = jax.nn.sigmoid(_dot(pm, gup_ref[i]) + gbias_ref[i])
        term = gate * _dot(y.astype(BF16), wbr_ref[i])
        acc = term if acc is None else acc + term
    o_ref[0] = acc.astype(BF16)


def _branch_call(p, y_a, y_b2, bonus, y_c, y_d2, ln_g, ln_b, d_norm, g_up, g_b, w_br, tm=256):
    B, T, _ = p.shape
    D = D_MODEL
    tok = pl.BlockSpec((1, tm, BW), lambda b, i: (b, i, 0))
    pcol = lambda off: pl.BlockSpec((1, tm, BW), lambda b, i: (b, i, off // BW))
    full = lambda shape: pl.BlockSpec(shape, lambda b, i: (0,) * len(shape))
    return pl.pallas_call(
        _branch_kernel,
        grid=(B, T // tm),
        in_specs=[pl.BlockSpec((1, tm, GATE_RANK), lambda b, i: (b, i, P_G // GATE_RANK)),
                  tok, tok, tok, tok, pcol(P_BG), tok, tok, tok, pcol(P_DG),
                  full((2, BW)), full((1, BW)),
                  full((N_BRANCH, GATE_RANK, D)), full((N_BRANCH, 1, D)), full((N_BRANCH, BW, D))],
        out_specs=pl.BlockSpec((1, tm, D), lambda b, i: (b, i, 0)),
        out_shape=jax.ShapeDtypeStruct((B, T, D), BF16),
        compiler_params=_cparams(("parallel", "parallel")),
        name="branch_merge",
    )(p, y_a, y_b2[0], y_b2[1], bonus, p, y_c, y_d2[0], y_d2[1], p, jnp.stack([ln_g, ln_b]),
      jnp.tile(d_norm, D_HEADS).reshape(1, BW), g_up, g_b.reshape(N_BRANCH, 1, D), w_br)


def _outproj_kernel(x_ref, acc_ref, w_ref, mod_ref, fg_ref, o_ref, *, tm, tile_off, final):
    row = (pl.program_id(1) + tile_off) * tm + lax.broadcasted_iota(jnp.int32, (tm, 1), 0)
    m = mod_ref[0]
    gate = jnp.where(row < CTX, m[5:6], m[4:5])
    y = x_ref[0] + gate * _dot(acc_ref[0], w_ref[...])
    if final:
        ms = jnp.mean(y * y, axis=-1, keepdims=True)
        y = y * lax.rsqrt(ms + EPS) * fg_ref[...]
    o_ref[0] = y


def _outproj_call(x, acc, w_out, modv, final_g, final, tm=256):
    B, T, D = x.shape
    tile_off = CTX // tm if final else 0
    t_out = T - CTX if final else T
    tok_in = lambda b, i: (b, i + tile_off, 0)
    return pl.pallas_call(
        functools.partial(_outproj_kernel, tm=tm, tile_off=tile_off, final=final),
        grid=(B, t_out // tm),
        in_specs=[pl.BlockSpec((1, tm, D), tok_in),
                  pl.BlockSpec((1, tm, D), tok_in),
                  pl.BlockSpec((D, D), lambda b, i: (0, 0)),
                  pl.BlockSpec((1, 8, D), lambda b, i: (b, 0, 0)),
                  pl.BlockSpec((1, D), lambda b, i: (0, 0))],
        out_specs=pl.BlockSpec((1, tm, D), lambda b, i: (b, i, 0)),
        out_shape=jax.ShapeDtypeStruct((B, t_out, D), F32),
        compiler_params=_cparams(("parallel", "parallel")),
        name="outproj",
    )(x, acc, w_out, modv, final_g.reshape(1, D))


def _pad_w_in(w):
    oa, ob, oc, od = 0, 1280, 3456, 4736
    og = od + 2064
    D = w.shape[0]
    z = lambda n: jnp.zeros((D, n), w.dtype)
    a_q, a_kv, a_g = w[:, oa:oa + 512], w[:, oa + 512:oa + 768], w[:, oa + 768:oa + 1280]
    b_rkv, b_l, b_g = w[:, ob:ob + 1536], w[:, ob + 1536:ob + 1664], w[:, ob + 1664:ob + 2176]
    c_q, c_kv, c_g = w[:, oc:oc + 512], w[:, oc + 512:oc + 768], w[:, oc + 768:oc + 1280]
    d_qkv, d_ab, d_g = w[:, od:od + 1536], w[:, od + 1536:od + 1552], w[:, od + 1552:od + 2064]
    g = w[:, og:og + GATE_RANK]
    cols = [a_q, a_g, c_q, c_g, b_g, d_g, b_rkv, d_qkv, a_kv, c_kv, g, b_l, d_ab, z(112)]
    return jnp.concatenate(cols, axis=1).astype(BF16)


def _rope_tables(n):
    rows = n // GRID_W
    row = jnp.repeat(jnp.arange(rows, dtype=F32), GRID_W)
    col = jnp.tile(jnp.arange(GRID_W, dtype=F32), rows)
    half = HD // 2
    inv = ROPE_THETA ** (-jnp.arange(0, half, 2, dtype=F32) / half)
    ar = row[:, None] * inv
    ac = col[:, None] * inv
    cos = jnp.concatenate([jnp.cos(ar), jnp.cos(ar), jnp.cos(ac), jnp.cos(ac)], axis=-1)
    sin = jnp.concatenate([-jnp.sin(ar), jnp.sin(ar), -jnp.sin(ac), jnp.sin(ac)], axis=-1)
    cos = jnp.concatenate([jnp.ones((CTX, HD), F32), cos], axis=0)
    sin = jnp.concatenate([jnp.zeros((CTX, HD), F32), sin], axis=0)
    return jnp.tile(cos, (1, 2)), jnp.tile(sin, (1, 2))


def kernel(x, c, ctx, c_ctx, norm_g, w_mod, b_mod, w_in, a_sink, b_mu, b_w0, b_wup, b_a0, b_aup,
           b_kk, b_ka, b_rk, b_lng, b_lnb, c_qn, c_kn, d_conv, d_alog, d_dtb, d_norm, g_up, g_b,
           w_br, w_out, final_g):
    B, n, D = x.shape
    L = w_mod.shape[0]
    T = CTX + n
    xa = jnp.concatenate([ctx, x], axis=1)
    cs = jnp.zeros((8, D), F32).at[:B].set(c).at[B].set(c_ctx)
    mod = _mod_call(cs, w_mod, b_mod)
    cos, sin = _rope_tables(n)
    ones = jnp.ones((1, 128), F32)
    tm_in = 768 if T % 768 == 0 else 256
    for l in range(L):
        shift, scale, gate = jnp.split(mod[l], 3, axis=-1)
        bc = lambda v: jnp.broadcast_to(v[B], (B, D))
        modv = jnp.stack([scale[:B], shift[:B], bc(scale), bc(shift), gate[:B], bc(gate),
                          jnp.zeros((B, D), F32), jnp.zeros((B, D), F32)], axis=1)
        p = _inproj_call(xa, modv, norm_g[l], _pad_w_in(w_in[l]), tm=tm_in)
        qz, k, v = _attn_prep_call(p, cos, sin, ones, ones, P_AQ, P_AKV, False, False)
        y_a = _flash_w_call(a_sink[l], qz, k, v, p, P_AG)
        qz, k, vt = _attn_prep_call(p, cos, sin, jnp.tile(c_qn[l], 2)[None], jnp.tile(c_kn[l], 2)[None],
                                    P_CQ, P_CKV, True, True)
        y_c = _flash_t_call(qz, k, vt, p, P_CG)
        r, vb, kk, lw, a, kt, bonus = _rwkv_feat_call(p, b_mu[l], b_w0[l], b_wup[l], b_a0[l], b_aup[l],
                                                      b_kk[l], b_ka[l], b_rk[l])
        y_b2 = _rwkv_call(r, vb, kk, lw, a, kt)
        qd, kd, vd, gb = _gdn_feat_call(p, d_conv[l], d_alog[l], d_dtb[l])
        y_d2 = _gdn_call(qd, kd, vd, gb)
        acc = _branch_call(p, y_a, y_b2, bonus, y_c, y_d2, b_lng[l], b_lnb[l], d_norm[l],
                           g_up[l].astype(BF16), g_b[l], w_br[l].astype(BF16))
        xa = _outproj_call(xa, acc, w_out[l].astype(BF16), modv, final_g, l == L - 1)
    return xa
```

```python
import functools
import math

import jax
import jax.numpy as jnp
from jax import lax
from jax.experimental import pallas as pl
from jax.experimental.pallas import tpu as pltpu

F32 = jnp.float32
BF16 = jnp.bfloat16
HI = lax.Precision.HIGHEST

D_MODEL = 2048
DEPTH = 4
GRID_W = 64
CTX = 256
N_BRANCH = 4
BW = D_MODEL // 4
HD = 64
N_HEADS = BW // HD
N_KV = 2
KVW = N_KV * HD
WINDOW = 128
B_LORA = 64
B_DECAY_SCALE = 0.606531
B_GN_EPS = 64e-5
D_HD = 128
D_HEADS = BW // D_HD
D_CONV = 5
GATE_RANK = D_MODEL // 8
ROPE_THETA = 10000.0
EPS = 1e-6
NEG = -1e30
CHUNK = 64

P_AQ, P_AG, P_CQ, P_CG, P_BG, P_DG = 0, 512, 1024, 1536, 2048, 2560
P_BRKV, P_DQKV = 3072, 4608
P_AKV, P_CKV, P_G = 6144, 6400, 6656
P_BL, P_DAB = 6912, 7040
P_W = 7168

VMEM_LIMIT = 56 * 1024 * 1024


def _cparams(sem):
    return pltpu.CompilerParams(dimension_semantics=sem, vmem_limit_bytes=VMEM_LIMIT)


def _dot(a, b, prec=None):
    return jnp.dot(a, b, precision=prec, preferred_element_type=F32)


def _dot_nt(a, b, prec=None):
    return lax.dot_general(a, b, (((1,), (1,)), ((), ())), precision=prec,
                           preferred_element_type=F32)


def _dot_tn(a, b, prec=None):
    return lax.dot_general(a, b, (((0,), (0,)), ((), ())), precision=prec,
                           preferred_element_type=F32)


def _mod_kernel(c_ref, w_ref, b_ref, o_ref):
    c = c_ref[...]
    s = c * jax.nn.sigmoid(c)
    o_ref[0] = _dot(s, w_ref[0], HI) + b_ref[0]


def _mod_call(cs, w_mod, b_mod):
    L, D, D3 = w_mod.shape
    tn = 768
    return pl.pallas_call(
        _mod_kernel,
        grid=(L, D3 // tn),
        in_specs=[pl.BlockSpec((8, D), lambda l, j: (0, 0)),
                  pl.BlockSpec((1, D, tn), lambda l, j: (l, 0, j)),
                  pl.BlockSpec((1, 1, tn), lambda l, j: (l, 0, j))],
        out_specs=pl.BlockSpec((1, 8, tn), lambda l, j: (l, 0, j)),
        out_shape=jax.ShapeDtypeStruct((L, 8, D3), F32),
        compiler_params=_cparams(("parallel", "parallel")),
        name="mod",
    )(cs, w_mod, b_mod.reshape(L, 1, D3))


def _inproj_kernel(x_ref, mod_ref, g_ref, w_ref, o_ref, h_ref, *, tm):
    @pl.when(pl.program_id(2) == 0)
    def _():
        rb = 16

        def body(r, carry):
            r0 = pl.multiple_of(r * rb, rb)
            x = x_ref[0, pl.ds(r0, rb), :]
            ms = jnp.mean(x * x, axis=-1, keepdims=True)
            y = x * lax.rsqrt(ms + EPS) * g_ref[...]
            is_ctx = pl.program_id(1) * tm + r0 < CTX
            scale = jnp.where(is_ctx, mod_ref[0, 2:3], mod_ref[0, 0:1])
            shift = jnp.where(is_ctx, mod_ref[0, 3:4], mod_ref[0, 1:2])
            h_ref[pl.ds(r0, rb), :] = (y * (1.0 + scale) + shift).astype(BF16)
            return carry

        lax.fori_loop(0, tm // rb, body, 0, unroll=4)

    o_ref[0] = _dot(h_ref[...], w_ref[...])


def _inproj_call(x, modv, norm_g, w_pad, tm=768, tn=512):
    B, T, D = x.shape
    return pl.pallas_call(
        functools.partial(_inproj_kernel, tm=tm),
        grid=(B, T // tm, P_W // tn),
        in_specs=[pl.BlockSpec((1, tm, D), lambda b, i, j: (b, i, 0)),
                  pl.BlockSpec((1, 8, D), lambda b, i, j: (b, 0, 0)),
                  pl.BlockSpec((1, D), lambda b, i, j: (0, 0)),
                  pl.BlockSpec((D, tn), lambda b, i, j: (0, j))],
        out_specs=pl.BlockSpec((1, tm, tn), lambda b, i, j: (b, i, j)),
        out_shape=jax.ShapeDtypeStruct((B, T, P_W), F32),
        scratch_shapes=[pltpu.VMEM((tm, D), BF16)],
        compiler_params=_cparams(("parallel", "parallel", "arbitrary")),
        name="inproj",
    )(x, modv, norm_g.reshape(1, D), w_pad)


def _rope128(x, cos, sin_signed):
    lane = lax.broadcasted_iota(jnp.int32, x.shape, 1)
    lo = (lane % 32) < 16
    rot = jnp.where(lo, pltpu.roll(x, 128 - 16, 1), pltpu.roll(x, 16, 1))
    return x * cos + rot * sin_signed


def _head_ms(x):
    r = lax.broadcasted_iota(jnp.int32, (128, 128), 0) // HD
    c = lax.broadcasted_iota(jnp.int32, (128, 128), 1) // HD
    ones = jnp.where(r == c, 1.0 / HD, 0.0).astype(F32)
    return _dot(x * x, ones, HI)


def _attn_prep_kernel(q_ref, kv_ref, cos_ref, sin_ref, qn_ref, kn_ref, qz_ref, k_ref, v_ref,
                      *, use_norm, v_transposed):
    cos = cos_ref[...]
    sin = sin_ref[...]
    tm = cos.shape[0]
    lane = lax.broadcasted_iota(jnp.int32, (tm, 128), 1)
    scale = HD ** -0.5 * LOG2E
    kv = kv_ref[0]
    k = kv[:, :KVW]
    if use_norm:
        k = k * lax.rsqrt(_head_ms(k) + EPS) * kn_ref[...]
    k_ref[0] = _rope128(k, cos, sin).astype(BF16)
    if v_transposed:
        v_ref[0, 0] = kv[:, KVW:].T.astype(BF16)
    else:
        v_ref[0] = kv[:, KVW:].astype(BF16)
    q = q_ref[0]
    for p in range(N_HEADS // 2):
        qp = q[:, 128 * p:128 * (p + 1)]
        if use_norm:
            qp = qp * lax.rsqrt(_head_ms(qp) + EPS) * qn_ref[...]
        qp = _rope128(qp, cos, sin) * scale
        g = (2 * p) // (N_HEADS // N_KV)
        other = pltpu.roll(qp, 64, 1)
        if g == 0:
            h0 = jnp.where(lane < 64, qp, 0.0)
            h1 = jnp.where(lane < 64, other, 0.0)
        else:
            h0 = jnp.where(lane >= 64, other, 0.0)
            h1 = jnp.where(lane >= 64, qp, 0.0)
        qz_ref[0, :, 256 * p:256 * p + 128] = h0.astype(BF16)
        qz_ref[0, :, 256 * p + 128:256 * p + 256] = h1.astype(BF16)


def _attn_prep_call(p, cos, sin, qn, kn, q_off, kv_off, use_norm, v_transposed):
    B, T, _ = p.shape
    tm = KC
    if v_transposed:
        v_spec = pl.BlockSpec((1, 1, KVW, tm), lambda b, i: (b, i, 0, 0))
        v_shape = jax.ShapeDtypeStruct((B, T // tm, KVW, tm), BF16)
    else:
        v_spec = pl.BlockSpec((1, tm, KVW), lambda b, i: (b, i, 0))
        v_shape = jax.ShapeDtypeStruct((B, T, KVW), BF16)
    return pl.pallas_call(
        functools.partial(_attn_prep_kernel, use_norm=use_norm, v_transposed=v_transposed),
        grid=(B, T // tm),
        in_specs=[pl.BlockSpec((1, tm, BW), lambda b, i: (b, i, q_off // BW)),
                  pl.BlockSpec((1, tm, 2 * KVW), lambda b, i: (b, i, kv_off // (2 * KVW))),
                  pl.BlockSpec((tm, 128), lambda b, i: (i, 0)),
                  pl.BlockSpec((tm, 128), lambda b, i: (i, 0)),
                  pl.BlockSpec((1, 128), lambda b, i: (0, 0)),
                  pl.BlockSpec((1, 128), lambda b, i: (0, 0))],
        out_specs=[pl.BlockSpec((1, tm, N_HEADS * 128), lambda b, i: (b, i, 0)),
                   pl.BlockSpec((1, tm, KVW), lambda b, i: (b, i, 0)),
                   v_spec],
        out_shape=[jax.ShapeDtypeStruct((B, T, N_HEADS * 128), BF16),
                   jax.ShapeDtypeStruct((B, T, KVW), BF16),
                   v_shape],
        compiler_params=_cparams(("parallel", "parallel")),
        name="attn_prep",
    )(p, p, cos, sin, qn, kn)


LOG2E = 1.4426950408889634
KC = 256


def _flash_w_kernel(sink_ref, qz_ref, k_ref, v_ref, g_ref, o_ref, m_ref, l_ref, acc_ref, *, tq):
    qi = pl.program_id(1)
    T = k_ref.shape[1]

    for h in range(N_HEADS):
        m_ref[h] = jnp.full((tq, 128), sink_ref[h] * LOG2E, F32)
        l_ref[h] = jnp.ones((tq, 128), F32)
    acc_ref[...] = jnp.zeros_like(acc_ref)

    def update(starts, masks):
        H = range(N_HEADS)
        ks = [k_ref[0, pl.ds(s, KC), :] for s in starts]
        vs = [v_ref[0, pl.ds(s, KC), :] for s in starts]
        s = [[_dot_nt(qz_ref[0, :, 128 * h:128 * (h + 1)], kc) for kc in ks] for h in H]
        m_new, alpha = [], []
        for h in H:
            mx = None
            for i, mk in enumerate(masks):
                if mk is not None:
                    s[h][i] = jnp.where(mk, s[h][i], NEG)
                mx = s[h][i] if mx is None else jnp.maximum(mx, s[h][i])
            m_prev = m_ref[h]
            m_new.append(jnp.maximum(m_prev, jnp.max(mx, axis=-1, keepdims=True)))
            alpha.append(jnp.exp2(m_prev - m_new[h]))
        for h in H:
            m_wide = jnp.concatenate([m_new[h]] * (KC // 128), axis=1)
            lsum = None
            pv = None
            for sc, vc in zip(s[h], vs):
                pc = jnp.exp2(sc - m_wide)
                lsum = pc if lsum is None else lsum + pc
                d = _dot(pc.astype(BF16), vc)
                pv = d if pv is None else pv + d
            m_ref[h] = m_new[h]
            l_ref[h] = alpha[h] * l_ref[h] + jnp.sum(lsum, axis=-1, keepdims=True)
            acc_ref[h] = alpha[h] * acc_ref[h] + pv

    ctx_starts = [KC * c for c in range(CTX // KC)]

    @pl.when(qi == 0)
    def _():
        update(ctx_starts, [None] * len(ctx_starts))

    @pl.when(qi > 0)
    def _():
        row = lax.broadcasted_iota(jnp.int32, (tq, KC), 0)
        col = lax.broadcasted_iota(jnp.int32, (tq, KC), 1)
        t0 = qi * tq
        starts, masks = [], []
        for c in range((tq + 2 * WINDOW) // KC):
            k0 = t0 - WINDOW + KC * c
            k0c = jnp.minimum(k0, T - KC)
            kpos = k0c + col
            dist = t0 + row - kpos
            masks.append((jnp.abs(dist) <= WINDOW) & (kpos >= jnp.maximum(k0, CTX)))
            starts.append(pl.multiple_of(k0c, WINDOW))
        update(ctx_starts + starts, [None] * len(ctx_starts) + masks)

    lane = lax.broadcasted_iota(jnp.int32, (tq, 128), 1)
    for p in range(N_HEADS // 2):
        g = (2 * p) // (N_HEADS // N_KV)
        a0 = acc_ref[2 * p] / l_ref[2 * p]
        a1 = acc_ref[2 * p + 1] / l_ref[2 * p + 1]
        if g == 0:
            o = jnp.where(lane < 64, a0, pltpu.roll(a1, 64, 1))
        else:
            o = jnp.where(lane < 64, pltpu.roll(a0, 64, 1), a1)
        gate = g_ref[0, :, 128 * p:128 * (p + 1)]
        o_ref[0, :, 128 * p:128 * (p + 1)] = o * (gate * jax.nn.sigmoid(gate))


def _flash_w_call(sink, qz, k, v, p, g_off):
    B, T, _ = qz.shape
    tq = CTX
    return pl.pallas_call(
        functools.partial(_flash_w_kernel, tq=tq),
        grid=(B, T // tq),
        in_specs=[pl.BlockSpec(memory_space=pltpu.SMEM),
                  pl.BlockSpec((1, tq, N_HEADS * 128), lambda b, i: (b, i, 0)),
                  pl.BlockSpec((1, T, KVW), lambda b, i: (b, 0, 0)),
                  pl.BlockSpec((1, T, KVW), lambda b, i: (b, 0, 0)),
                  pl.BlockSpec((1, tq, BW), lambda b, i: (b, i, g_off // BW))],
        out_specs=pl.BlockSpec((1, tq, BW), lambda b, i: (b, i, 0)),
        out_shape=jax.ShapeDtypeStruct((B, T, BW), F32),
        scratch_shapes=[pltpu.VMEM((N_HEADS, tq, 128), F32),
                        pltpu.VMEM((N_HEADS, tq, 128), F32),
                        pltpu.VMEM((N_HEADS, tq, 128), F32)],
        compiler_params=_cparams(("parallel", "parallel")),
        name="flash_window",
    )(sink, qz, k, v, p)


def _flash_t_kernel(qz_ref, k_ref, vt_ref, g_ref, o_ref, m_ref, l_ref, acc_ref, *, tq, tk):
    qi = pl.program_id(1)
    T = k_ref.shape[1]
    per = N_HEADS // N_KV

    m_ref[...] = jnp.full(m_ref.shape, NEG, F32)
    l_ref[...] = jnp.zeros(l_ref.shape, F32)
    acc_ref[...] = jnp.zeros_like(acc_ref)

    def update(chunks):
        H = range(N_HEADS)
        ks = [k_ref[0, pl.ds(pl.multiple_of(c * KC, KC), KC), :] for c in chunks]
        s = [[_dot_nt(kc, qz_ref[0, :, 128 * h:128 * (h + 1)]) for kc in ks] for h in H]
        m_new, alpha = [], []
        for h in H:
            mx = s[h][0]
            for sc in s[h][1:]:
                mx = jnp.maximum(mx, sc)
            m_prev = m_ref[h, 0:1]
            m_new.append(jnp.maximum(m_prev, jnp.max(mx, axis=0, keepdims=True)))
            alpha.append(jnp.exp2(m_prev - m_new[h]))
        for h in H:
            g = h // per
            lsum = None
            pv = None
            for sc, c in zip(s[h], chunks):
                pc = jnp.exp2(sc - m_new[h])
                lsum = pc if lsum is None else lsum + pc
                d = _dot(vt_ref[0, c, HD * g:HD * (g + 1), :], pc.astype(BF16))
                pv = d if pv is None else pv + d
            m_ref[h] = jnp.broadcast_to(m_new[h], (8, tq))
            l_ref[h] = jnp.broadcast_to(alpha[h] * l_ref[h, 0:1] + jnp.sum(lsum, axis=0, keepdims=True), (8, tq))
            acc_ref[h] = alpha[h] * acc_ref[h] + pv

    @pl.when(qi == 0)
    def _():
        update([c for c in range(CTX // KC)])

    @pl.when(qi > 0)
    def _():
        def body(t, carry):
            update([t * (tk // KC) + c for c in range(tk // KC)])
            return carry
        lax.fori_loop(0, T // tk, body, 0)

    o_t = jnp.concatenate([acc_ref[h] / l_ref[h, 0:1] for h in range(N_HEADS)], axis=0)
    gate = g_ref[0]
    o_ref[0] = o_t.T * (gate * jax.nn.sigmoid(gate))


def _flash_t_call(qz, k, vt, p, g_off, tk=768):
    B, T, _ = qz.shape
    tq = CTX
    if T % tk:
        tk = KC
    return pl.pallas_call(
        functools.partial(_flash_t_kernel, tq=tq, tk=tk),
        grid=(B, T // tq),
        in_specs=[pl.BlockSpec((1, tq, N_HEADS * 128), lambda b, i: (b, i, 0)),
                  pl.BlockSpec((1, T, KVW), lambda b, i: (b, 0, 0)),
                  pl.BlockSpec((1, T // KC, KVW, KC), lambda b, i: (b, 0, 0, 0)),
                  pl.BlockSpec((1, tq, BW), lambda b, i: (b, i, g_off // BW))],
        out_specs=pl.BlockSpec((1, tq, BW), lambda b, i: (b, i, 0)),
        out_shape=jax.ShapeDtypeStruct((B, T, BW), F32),
        scratch_shapes=[pltpu.VMEM((N_HEADS, 8, tq), F32),
                        pltpu.VMEM((N_HEADS, 8, tq), F32),
                        pltpu.VMEM((N_HEADS, HD, tq), F32)],
        compiler_params=_cparams(("parallel", "parallel")),
        name="flash_global",
    )(qz, k, vt, p)


def _chunk_masks(d):
    row = lax.broadcasted_iota(jnp.int32, (CHUNK, 2 * CHUNK), 0)
    col = lax.broadcasted_iota(jnp.int32, (CHUNK, 2 * CHUNK), 1) % CHUNK
    lag = (row - col) * (1 - 2 * d)
    return row, col, lag > 0, lag >= 0


def _hl(x):
    hi = x.astype(BF16)
    return hi, (x - hi.astype(F32)).astype(BF16)


def _lhs_tok(m2):
    return jnp.concatenate(_hl(m2), axis=1)


def _rhs_tok(x):
    xh, xl = _hl(x)
    return jnp.concatenate([xh, xl, xh, xl], axis=0)


def _mm_tok(m2, x):
    return _dot(_lhs_tok(m2), _rhs_tok(x))


def _mm_nt3(a, b):
    ah, al = _hl(a)
    bh, bl = _hl(b)
    return (_dot_nt(jnp.concatenate([ah, al], axis=1), jnp.concatenate([bh, bh], axis=1))
            + _dot_nt(ah, bl))


def _mm_ch3(a, s):
    ah, al = _hl(a)
    sh, sl = _hl(s)
    return (_dot(jnp.concatenate([ah, al], axis=1), jnp.concatenate([sh, sh], axis=0))
            + _dot(ah, sl))


def _mm_tn(a, b):
    ah, al = _hl(a)
    bh, bl = _hl(b)
    at = jnp.concatenate([ah.astype(F32), al.astype(F32)], axis=0).T.astype(BF16)
    return _dot(jnp.concatenate([at, at], axis=1), jnp.concatenate([bh, bh, bl, bl], axis=0))


def _unit_tri_inverse(a2, row, col):
    n = len(a2)
    eye = (row == col).astype(F32)
    t2 = [eye - jnp.where((row ^ col) == 1, a, 0.0) for a in a2]
    s = 2
    while s < CHUNK:
        off = ((row ^ col) // s) == 1
        hl = [_hl(t) for t in t2]
        y2 = [_dot(_lhs_tok(jnp.where(off, a2[i], 0.0)),
                   jnp.concatenate([hl[i][0], hl[i][1], hl[i][0], hl[i][1]], axis=0)) for i in range(n)]
        z2 = [_dot(jnp.concatenate(hl[i], axis=1), _rhs_tok(y2[i])) for i in range(n)]
        t2 = [t2[i] - z2[i] for i in range(n)]
        s *= 2
    return t2


def _scan_chunk_index(p, d, n_ctx_chunks, n_chunks):
    rev_idx = jnp.where(p < n_ctx_chunks, n_ctx_chunks - 1 - p, n_chunks - 1 + n_ctx_chunks - p)
    return jnp.where(d == 0, p, rev_idx)


def _cumsum_tok(incl, x):
    x1 = x.astype(BF16)
    r1 = x - x1.astype(F32)
    x2 = r1.astype(BF16)
    x3 = (r1 - x2.astype(F32)).astype(BF16)
    m = incl.astype(BF16)
    return _dot(jnp.concatenate([m, m, m], axis=1), jnp.concatenate([x1, x2, x3], axis=0))


def _rwkv_kernel(rf_ref, vf_ref, kkf_ref, rr_ref, vr_ref, kkr_ref, lw_ref, a_ref, kt_ref,
                 lwr_ref, ar_ref, ktr_ref, yf_ref, yr_ref, s_ref):
    @pl.when(pl.program_id(1) == 0)
    def _():
        s_ref[...] = jnp.zeros_like(s_ref)

    C = CHUNK
    NP = N_HEADS // 2
    lane = lax.broadcasted_iota(jnp.int32, (C, 128), 1)
    even = lane < HD
    r128 = lax.broadcasted_iota(jnp.int32, (128, 128), 0)
    c128 = lax.broadcasted_iota(jnp.int32, (128, 128), 1)
    blockdiag = (r128 // HD) == (c128 // HD)
    sls = [slice(128 * p, 128 * (p + 1)) for p in range(NP)]

    kp, rg, kd, ad, v, el, strict, incl = [], [], [], [], [], [], [], []
    row = col = None
    for d, (r_ref, v_ref, kk_ref, w_ref, aa_ref, k_ref) in enumerate(
            ((rf_ref, vf_ref, kkf_ref, lw_ref, a_ref, kt_ref), (rr_ref, vr_ref, kkr_ref, lwr_ref, ar_ref, ktr_ref))):
        row, col, strict_d, incl_d = _chunk_masks(d)
        lw = w_ref[0, 0]
        g = _cumsum_tok(incl_d[:, :C], lw)
        egl = jnp.exp(jnp.sum(lw, axis=0, keepdims=True))
        kk = kk_ref[0]
        ieg = jnp.exp(-g)
        kp_all = kk * jnp.exp(g - lw)
        rg_all = r_ref[0] * jnp.exp(g)
        kd_all = k_ref[0, 0] * ieg
        ad_all = kk * aa_ref[0, 0] * ieg
        v_all = v_ref[0]
        for sl in sls:
            kp.append(kp_all[:, sl])
            rg.append(rg_all[:, sl])
            kd.append(kd_all[:, sl])
            ad.append(ad_all[:, sl])
            v.append(v_all[:, sl])
            el.append(egl[:, sl])
            strict.append(strict_d)
            incl.append(incl_d)
    U = range(2 * NP)
    gram = [_mm_nt3(jnp.concatenate([jnp.where(even, kp[u], 0.0), jnp.where(even, rg[u], 0.0),
                                     jnp.where(even, 0.0, kp[u]), jnp.where(even, 0.0, rg[u])], axis=0),
                    jnp.concatenate([kd[u], kd[u], ad[u], ad[u]], axis=0)) for u in U]
    heads = [(u, e) for u in U for e in range(2)]
    gk = [gram[u][2 * C * e:2 * C * e + C] for u, e in heads]
    gr = [gram[u][2 * C * e + C:2 * C * (e + 1)] for u, e in heads]
    t2 = _unit_tri_inverse([jnp.where(strict[u], gk[i][:, 2 * C:], 0.0) for i, (u, e) in enumerate(heads)],
                           row, col)
    v_rhs = [_rhs_tok(x) for x in v]
    xy = [_dot(_lhs_tok(jnp.concatenate([jnp.where(strict[u], gk[i][:, :2 * C], 0.0),
                                         jnp.where(incl[u], gr[i][:, :2 * C], 0.0)], axis=0)), v_rhs[u])
          for i, (u, e) in enumerate(heads)]
    sol = [_mm_tok(t2[i], jnp.concatenate([kp[u], xy[i][:C]], axis=1))
           for i, (u, e) in enumerate(heads)]
    s0 = [s_ref[u] for u in U]
    rd = [_mm_nt3(jnp.concatenate([jnp.where(even, sol[2 * u][:, :128], sol[2 * u + 1][:, :128]), rg[u]],
                                  axis=0), s0[u]) for u in U]
    uu = [jnp.where(even, sol[2 * u][:, 128:], sol[2 * u + 1][:, 128:]) + rd[u][:C] for u in U]
    u_rhs = [_rhs_tok(x) for x in uu]
    bu = [_dot(_lhs_tok(jnp.where(incl[u], gr[i][:, 2 * C:], 0.0)), u_rhs[u]) for i, (u, e) in enumerate(heads)]
    for u in U:
        y_ref = yf_ref if u < NP else yr_ref
        y_ref[0, :, sls[u % NP]] = (rd[u][C:] + jnp.where(even, xy[2 * u][C:], xy[2 * u + 1][C:])
                                    - jnp.where(even, bu[2 * u], bu[2 * u + 1]))
    upd = [_mm_tn(v[u], kd[u] * el[u]) - _mm_tn(uu[u], ad[u] * el[u]) for u in U]
    for u in U:
        s_ref[u] = s0[u] * el[u] + jnp.where(blockdiag, upd[u], 0.0)


def _scan_specs(nc, width):
    ncc = CTX // CHUNK
    fwd = pl.BlockSpec((1, CHUNK, width), lambda b, p: (b, p, 0))
    rev = pl.BlockSpec((1, CHUNK, width), lambda b, p: (b, _scan_chunk_index(p, 1, ncc, nc), 0))
    fwd_d = pl.BlockSpec((1, 1, CHUNK, width), lambda b, p: (0, b, p, 0))
    rev_d = pl.BlockSpec((1, 1, CHUNK, width), lambda b, p: (1, b, _scan_chunk_index(p, 1, ncc, nc), 0))
    return fwd, rev, fwd_d, rev_d


def _rwkv_call(r, v, kk, lw, a, kt):
    B, T, W = r.shape
    nc = T // CHUNK
    fwd, rev, fwd_d, rev_d = _scan_specs(nc, W)
    return pl.pallas_call(
        _rwkv_kernel,
        grid=(B, nc),
        in_specs=[fwd] * 3 + [rev] * 3 + [fwd_d] * 3 + [rev_d] * 3,
        out_specs=[fwd, rev],
        out_shape=[jax.ShapeDtypeStruct((B, T, W), F32)] * 2,
        scratch_shapes=[pltpu.VMEM((N_HEADS, 128, 128), F32)],
        compiler_params=_cparams(("parallel", "arbitrary")),
        name="rwkv_scan",
    )(r, v, kk, r, v, kk, lw, a, kt, lw, a, kt)


def _gdn_kernel(qf_ref, kf_ref, vf_ref, gbf_ref, qr_ref, kr_ref, vr_ref, gbr_ref, of_ref, or_ref, s_ref):
    @pl.when(pl.program_id(1) == 0)
    def _():
        s_ref[...] = jnp.zeros_like(s_ref)

    C = CHUNK
    sls = [slice(D_HD * h, D_HD * (h + 1)) for h in range(D_HEADS)]
    q, k, v, gc, gl, beta, dmask, strict = [], [], [], [], [], [], [], []
    row = col = None
    for d, (q_ref, k_ref, v_ref, gb_ref) in enumerate(((qf_ref, kf_ref, vf_ref, gbf_ref),
                                                        (qr_ref, kr_ref, vr_ref, gbr_ref))):
        row, col, strict_d, incl_d = _chunk_masks(d)
        gb = gb_ref[0]
        gc_all = _cumsum_tok(incl_d[:, :C], gb)
        incl_t = ((row - col) * (1 - 2 * d) <= 0).astype(F32)
        gct_all = _dot_tn(gb, incl_t, HI)
        gl_all = jnp.sum(gb, axis=0, keepdims=True)
        for h in range(D_HEADS):
            ln = 2 * D_HEADS * d + h
            q.append(q_ref[0, :, sls[h]])
            k.append(k_ref[0, :, sls[h]])
            v.append(v_ref[0, :, sls[h]])
            gc.append(gc_all[:, ln:ln + 1])
            gl.append(gl_all[:, ln:ln + 1])
            beta.append(gb[:, ln + D_HEADS:ln + D_HEADS + 1])
            dmask.append(jnp.where(incl_d, jnp.exp(jnp.where(incl_d, gc[-1] - gct_all[ln:ln + 1, :], 0.0)), 0.0))
            strict.append(strict_d)
    U = range(2 * D_HEADS)
    kb = [k[u] * beta[u] for u in U]
    eg = [jnp.exp(gc[u]) for u in U]
    gram = [_mm_nt3(jnp.concatenate([kb[u], q[u]], axis=0), jnp.concatenate([k[u], k[u]], axis=0)) for u in U]
    t2 = _unit_tri_inverse([jnp.where(strict[u], gram[u][:C] * dmask[u], 0.0) for u in U], row, col)
    sol = [_mm_tok(t2[u], jnp.concatenate([v[u] * beta[u], kb[u] * eg[u]], axis=1)) for u in U]
    s0 = [s_ref[u] for u in U]
    rd = [_mm_ch3(jnp.concatenate([sol[u][:, D_HD:], q[u] * eg[u]], axis=0), s0[u]) for u in U]
    vnew = [sol[u][:, :D_HD] - rd[u][:C] for u in U]
    qv = [_mm_tok(gram[u][C:] * dmask[u], vnew[u]) for u in U]
    for u in U:
        o_ref = of_ref if u < D_HEADS else or_ref
        o_ref[0, :, sls[u % D_HEADS]] = rd[u][C:] + qv[u]
    upd = [_mm_tn(k[u] * jnp.exp(gl[u] - gc[u]), vnew[u]) for u in U]
    for u in U:
        s_ref[u] = s0[u] * jnp.exp(gl[u]) + upd[u]


def _gdn_call(q, k, v, gb):
    B, T, W = q.shape
    nc = T // CHUNK
    fwd, rev, _, _ = _scan_specs(nc, W)
    fwd_g, rev_g, _, _ = _scan_specs(nc, 128)
    return pl.pallas_call(
        _gdn_kernel,
        grid=(B, nc),
        in_specs=[fwd] * 3 + [fwd_g] + [rev] * 3 + [rev_g],
        out_specs=[fwd, rev],
        out_shape=[jax.ShapeDtypeStruct((B, T, W), F32)] * 2,
        scratch_shapes=[pltpu.VMEM((2 * D_HEADS, D_HD, D_HD), F32)],
        compiler_params=_cparams(("parallel", "arbitrary")),
        name="gdn_scan",
    )(q, k, v, gb, q, k, v, gb)


FT = CTX


def _halo_specs(width, lane_block, n_tiles):
    rb = FT // 8
    return [pl.BlockSpec((1, FT, width), lambda b, i: (b, i, lane_block)),
            pl.BlockSpec((1, 8, width), lambda b, i: (b, jnp.maximum(i * rb - 1, 0), lane_block)),
            pl.BlockSpec((1, 8, width), lambda b, i: (b, jnp.minimum((i + 1) * rb, n_tiles * rb - 1), lane_block))]


def _halo_flags():
    i = pl.program_id(1)
    has_prev = (i >= 2).astype(F32)
    has_next = ((i >= 1) & (i < pl.num_programs(1) - 1)).astype(F32)
    return has_prev, has_next


def _shifted(z, k, prev8, next8, row):
    tm = z.shape[0]
    if k > 0:
        out = pltpu.roll(z, k, 0)
        for j in range(k):
            out = jnp.where(row == j, prev8[8 - k + j:8 - k + j + 1], out)
    else:
        out = pltpu.roll(z, tm + k, 0)
        for j in range(-k):
            out = jnp.where(row == tm + k + j, next8[j:j + 1], out)
    return out


def _group_mean(x, hd):
    r = lax.broadcasted_iota(jnp.int32, (128, 128), 0) // hd
    c = lax.broadcasted_iota(jnp.int32, (128, 128), 1) // hd
    return _dot(x, jnp.where(r == c, 1.0 / hd, 0.0).astype(F32), HI)


def _rwkv_feat_kernel(z_ref, zp_ref, zn_ref, l_ref, lp_ref, ln_ref, mu_ref, mul_ref, wl_ref, b0_ref,
                      kk_ref, ka_ref, rk_ref, r_out, v_out, kk_out, lw_out, a_out, kt_out, bon_out):
    W = BW
    has_prev, has_next = _halo_flags()
    row = lax.broadcasted_iota(jnp.int32, (FT, 1), 0)

    def token_shift(x_ref, xp_ref, xn_ref, m_ref):
        z = x_ref[0]
        prev = _shifted(z, 1, xp_ref[0] * has_prev, None, row)
        nxt = _shifted(z, -1, None, xn_ref[0] * has_next, row)
        return z + m_ref[0:1] * (prev - z) + m_ref[1:2] * (nxt - z)

    z = token_shift(z_ref, zp_ref, zn_ref, mu_ref)
    zl = token_shift(l_ref, lp_ref, ln_ref, mul_ref)
    r, k, v = z[:, :W], z[:, W:2 * W], z[:, 2 * W:]
    lane = lax.broadcasted_iota(jnp.int32, zl.shape, 1)
    zl = jnp.where(lane < B_LORA, jnp.tanh(zl), zl)
    r_out[0] = r
    v_out[0] = v
    kk = k * kk_ref[...]
    kks = []
    for c in range(W // 128):
        x = kk[:, 128 * c:128 * (c + 1)]
        kks.append(x * lax.rsqrt(jnp.maximum(_group_mean(x * x, HD) * HD, EPS * EPS)))
    kk_out[0] = jnp.concatenate(kks, axis=1)
    bonus = None
    for d in range(2):
        pre = _dot(zl, wl_ref[d], HI) + b0_ref[d]
        lw_out[d, 0] = -B_DECAY_SCALE * jax.nn.sigmoid(pre[:, :W])
        a = jax.nn.sigmoid(pre[:, W:])
        a_out[d, 0] = a
        kt = k * (1.0 + (a - 1.0) * ka_ref[...])
        kt_out[d, 0] = kt
        rkk = r * kt * rk_ref[...]
        b = jnp.concatenate([_group_mean(rkk[:, 128 * c:128 * (c + 1)], HD) * HD for c in range(W // 128)],
                            axis=1) * v
        bonus = b if bonus is None else bonus + b
    bon_out[0] = bonus


def _rwkv_feat_call(p, mu, w0, wup, a0, aup, k_k, k_a, r_k):
    B, T, _ = p.shape
    W = BW
    nt = T // FT
    zero = jnp.zeros((2, B_LORA, W), F32)
    wl = jnp.concatenate([jnp.concatenate([wup, zero], axis=2), jnp.concatenate([zero, aup], axis=2)], axis=1)
    b0 = jnp.concatenate([w0, a0], axis=1).reshape(2, 1, 2 * W)
    full = lambda shape: pl.BlockSpec(shape, lambda b, i: (0,) * len(shape))
    tok = pl.BlockSpec((1, FT, W), lambda b, i: (b, i, 0))
    tokd = pl.BlockSpec((2, 1, FT, W), lambda b, i: (0, b, i, 0))
    sd = jax.ShapeDtypeStruct
    return pl.pallas_call(
        _rwkv_feat_kernel,
        grid=(B, nt),
        in_specs=_halo_specs(3 * W, P_BRKV // (3 * W), nt) + _halo_specs(128, P_BL // 128, nt)
        + [full((2, 3 * W)), full((2, 128)), full((2, 128, 2 * W)), full((2, 1, 2 * W)),
           full((1, W)), full((1, W)), full((1, W))],
        out_specs=[tok, tok, tok, tokd, tokd, tokd, tok],
        out_shape=[sd((B, T, W), F32)] * 3 + [sd((2, B, T, W), F32)] * 3 + [sd((B, T, W), F32)],
        compiler_params=_cparams(("parallel", "parallel")),
        name="rwkv_feat",
    )(p, p, p, p, p, p, mu[:, :3 * W], mu[:, 3 * W:], wl, b0,
      k_k.reshape(1, W), k_a.reshape(1, W), r_k.reshape(1, W))


def _gdn_feat_kernel(z_ref, zp_ref, zn_ref, ab_ref, cw_ref, lp_ref, q_out, k_out, v_out, gb_out):
    W = BW
    has_prev, has_next = _halo_flags()
    row = lax.broadcasted_iota(jnp.int32, (FT, 1), 0)
    z = z_ref[0]
    prev8 = zp_ref[0] * has_prev
    next8 = zn_ref[0] * has_next
    half = D_CONV // 2
    conv = cw_ref[half:half + 1] * z
    for i in range(D_CONV):
        if i != half:
            conv = conv + cw_ref[i:i + 1] * _shifted(z, half - i, prev8, next8, row)
    qkv = conv * jax.nn.sigmoid(conv)

    def l2n(x):
        return x * lax.rsqrt(jnp.maximum(jnp.sum(x * x, axis=-1, keepdims=True), EPS * EPS))

    for h in range(D_HEADS):
        sl = slice(D_HD * h, D_HD * (h + 1))
        q_out[0, :, sl] = l2n(qkv[:, sl]) * (D_HD ** -0.5)
        k_out[0, :, sl] = l2n(qkv[:, W + D_HD * h:W + D_HD * (h + 1)])
    v_out[0] = qkv[:, 2 * W:]
    x = ab_ref[0]
    y = x + lp_ref[1:2]
    softplus = jnp.maximum(y, 0.0) + jnp.log1p(jnp.exp(-jnp.abs(y)))
    gb_out[0] = jnp.where(lp_ref[2:3] > 0.5, lp_ref[0:1] * softplus, jax.nn.sigmoid(x))


def _gdn_feat_call(p, conv_w, alog, dtb):
    B, T, _ = p.shape
    W, H = BW, D_HEADS
    nt = T // FT
    is_a = jnp.tile(jnp.concatenate([jnp.ones((H,), F32), jnp.zeros((H,), F32)]), 2)
    nea = jnp.concatenate([-jnp.exp(alog[0]), jnp.zeros((H,), F32), -jnp.exp(alog[1]), jnp.zeros((H,), F32)])
    dtl = jnp.concatenate([dtb[0], jnp.zeros((H,), F32), dtb[1], jnp.zeros((H,), F32)])
    lp = jnp.pad(jnp.stack([nea, dtl, is_a]), ((0, 5), (0, 128 - 4 * H)))
    cw = jnp.pad(conv_w, ((0, 8 - D_CONV), (0, 0)))
    full = lambda shape: pl.BlockSpec(shape, lambda b, i: (0,) * len(shape))
    tok = pl.BlockSpec((1, FT, W), lambda b, i: (b, i, 0))
    sd = jax.ShapeDtypeStruct
    return pl.pallas_call(
        _gdn_feat_kernel,
        grid=(B, nt),
        in_specs=_halo_specs(3 * W, P_DQKV // (3 * W), nt)
        + [pl.BlockSpec((1, FT, 128), lambda b, i: (b, i, P_DAB // 128)), full((8, 3 * W)), full((8, 128))],
        out_specs=[tok, tok, tok, pl.BlockSpec((1, FT, 128), lambda b, i: (b, i, 0))],
        out_shape=[sd((B, T, W), F32)] * 3 + [sd((B, T, 128), F32)],
        compiler_params=_cparams(("parallel", "parallel")),
        name="gdn_feat",
    )(p, p, p, p, cw, lp)


def _branch_kernel(pm_ref, ya_ref, yb0_ref, yb1_ref, bon_ref, gb_ref, yc_ref, yd0_ref, yd1_ref, gd_ref,
                   ln_ref, dn_ref, gup_ref, gbias_ref, wbr_ref, o_ref):
    def silu(x):
        return x * jax.nn.sigmoid(x)

    wkv = yb0_ref[0] + yb1_ref[0]
    gn = []
    for c in range(BW // 128):
        x = wkv[:, 128 * c:128 * (c + 1)]
        cen = x - _group_mean(x, HD)
        gn.append(cen * lax.rsqrt(_group_mean(cen * cen, HD) + B_GN_EPS))
    y_b = (jnp.concatenate(gn, axis=1) * ln_ref[0:1] + ln_ref[1:2] + bon_ref[0]) * silu(gb_ref[0])
    o = yd0_ref[0] + yd1_ref[0]
    on = []
    for h in range(D_HEADS):
        x = o[:, D_HD * h:D_HD * (h + 1)]
        on.append(x * lax.rsqrt(jnp.mean(x * x, axis=-1, keepdims=True) + EPS))
    y_d = jnp.concatenate(on, axis=1) * dn_ref[...] * silu(gd_ref[0])

    pm = pm_ref[0].astype(BF16)
    acc = None
    for i, y in enumerate((ya_ref[0], y_b, yc_ref[0], y_d)):
        gate = jax.nn.sigmoid(_dot(pm, gup_ref[i]) + gbias_ref[i])
        term = gate * _dot(y.astype(BF16), wbr_ref[i])
        acc = term if acc is None else acc + term
    o_ref[0] = acc.astype(BF16)


def _branch_call(p, y_a, y_b2, bonus, y_c, y_d2, ln_g, ln_b, d_norm, g_up, g_b, w_br, tm=256):
    B, T, _ = p.shape
    D = D_MODEL
    tok = pl.BlockSpec((1, tm, BW), lambda b, i: (b, i, 0))
    pcol = lambda off: pl.BlockSpec((1, tm, BW), lambda b, i: (b, i, off // BW))
    full = lambda shape: pl.BlockSpec(shape, lambda b, i: (0,) * len(shape))
    return pl.pallas_call(
        _branch_kernel,
        grid=(B, T // tm),
        in_specs=[pl.BlockSpec((1, tm, GATE_RANK), lambda b, i: (b, i, P_G // GATE_RANK)),
                  tok, tok, tok, tok, pcol(P_BG), tok, tok, tok, pcol(P_DG),
                  full((2, BW)), full((1, BW)),
                  full((N_BRANCH, GATE_RANK, D)), full((N_BRANCH, 1, D)), full((N_BRANCH, BW, D))],
        out_specs=pl.BlockSpec((1, tm, D), lambda b, i: (b, i, 0)),
        out_shape=jax.ShapeDtypeStruct((B, T, D), BF16),
        compiler_params=_cparams(("parallel", "parallel")),
        name="branch_merge",
    )(p, y_a, y_b2[0], y_b2[1], bonus, p, y_c, y_d2[0], y_d2[1], p, jnp.stack([ln_g, ln_b]),
      jnp.tile(d_norm, D_HEADS).reshape(1, BW), g_up, g_b.reshape(N_BRANCH, 1, D), w_br)


def _outproj_kernel(x_ref, acc_ref, w_ref, mod_ref, fg_ref, o_ref, *, tm, tile_off, final):
    row = (pl.program_id(1) + tile_off) * tm + lax.broadcasted_iota(jnp.int32, (tm, 1), 0)
    m = mod_ref[0]
    gate = jnp.where(row < CTX, m[5:6], m[4:5])
    y = x_ref[0] + gate * _dot(acc_ref[0], w_ref[...])
    if final:
        ms = jnp.mean(y * y, axis=-1, keepdims=True)
        y = y * lax.rsqrt(ms + EPS) * fg_ref[...]
    o_ref[0] = y


def _outproj_call(x, acc, w_out, modv, final_g, final, tm=256):
    B, T, D = x.shape
    tile_off = CTX // tm if final else 0
    t_out = T - CTX if final else T
    tok_in = lambda b, i: (b, i + tile_off, 0)
    return pl.pallas_call(
        functools.partial(_outproj_kernel, tm=tm, tile_off=tile_off, final=final),
        grid=(B, t_out // tm),
        in_specs=[pl.BlockSpec((1, tm, D), tok_in),
                  pl.BlockSpec((1, tm, D), tok_in),
                  pl.BlockSpec((D, D), lambda b, i: (0, 0)),
                  pl.BlockSpec((1, 8, D), lambda b, i: (b, 0, 0)),
                  pl.BlockSpec((1, D), lambda b, i: (0, 0))],
        out_specs=pl.BlockSpec((1, tm, D), lambda b, i: (b, i, 0)),
        out_shape=jax.ShapeDtypeStruct((B, t_out, D), F32),
        compiler_params=_cparams(("parallel", "parallel")),
        name="outproj",
    )(x, acc, w_out, modv, final_g.reshape(1, D))


def _pad_w_in(w):
    oa, ob, oc, od = 0, 1280, 3456, 4736
    og = od + 2064
    D = w.shape[0]
    z = lambda n: jnp.zeros((D, n), w.dtype)
    a_q, a_kv, a_g = w[:, oa:oa + 512], w[:, oa + 512:oa + 768], w[:, oa + 768:oa + 1280]
    b_rkv, b_l, b_g = w[:, ob:ob + 1536], w[:, ob + 1536:ob + 1664], w[:, ob + 1664:ob + 2176]
    c_q, c_kv, c_g = w[:, oc:oc + 512], w[:, oc + 512:oc + 768], w[:, oc + 768:oc + 1280]
    d_qkv, d_ab, d_g = w[:, od:od + 1536], w[:, od + 1536:od + 1552], w[:, od + 1552:od + 2064]
    g = w[:, og:og + GATE_RANK]
    cols = [a_q, a_g, c_q, c_g, b_g, d_g, b_rkv, d_qkv, a_kv, c_kv, g, b_l, d_ab, z(112)]
    return jnp.concatenate(cols, axis=1).astype(BF16)


def _rope_tables(n):
    rows = n // GRID_W
    row = jnp.repeat(jnp.arange(rows, dtype=F32), GRID_W)
    col = jnp.tile(jnp.arange(GRID_W, dtype=F32), rows)
    half = HD // 2
    inv = ROPE_THETA ** (-jnp.arange(0, half, 2, dtype=F32) / half)
    ar = row[:, None] * inv
    ac = col[:, None] * inv
    cos = jnp.concatenate([jnp.cos(ar), jnp.cos(ar), jnp.cos(ac), jnp.cos(ac)], axis=-1)
    sin = jnp.concatenate([-jnp.sin(ar), jnp.sin(ar), -jnp.sin(ac), jnp.sin(ac)], axis=-1)
    cos = jnp.concatenate([jnp.ones((CTX, HD), F32), cos], axis=0)
    sin = jnp.concatenate([jnp.zeros((CTX, HD), F32), sin], axis=0)
    return jnp.tile(cos, (1, 2)), jnp.tile(sin, (1, 2))


def kernel(x, c, ctx, c_ctx, norm_g, w_mod, b_mod, w_in, a_sink, b_mu, b_w0, b_wup, b_a0, b_aup,
           b_kk, b_ka, b_rk, b_lng, b_lnb, c_qn, c_kn, d_conv, d_alog, d_dtb, d_norm, g_up, g_b,
           w_br, w_out, final_g):
    B, n, D = x.shape
    L = w_mod.shape[0]
    T = CTX + n
    xa = jnp.concatenate([ctx, x], axis=1)
    cs = jnp.zeros((8, D), F32).at[:B].set(c).at[B].set(c_ctx)
    mod = _mod_call(cs, w_mod, b_mod)
    cos, sin = _rope_tables(n)
    ones = jnp.ones((1, 128), F32)
    tm_in = 768 if T % 768 == 0 else 256
    for l in range(L):
        shift, scale, gate = jnp.split(mod[l], 3, axis=-1)
        bc = lambda v: jnp.broadcast_to(v[B], (B, D))
        modv = jnp.stack([scale[:B], shift[:B], bc(scale), bc(shift), gate[:B], bc(gate),
                          jnp.zeros((B, D), F32), jnp.zeros((B, D), F32)], axis=1)
        p = _inproj_call(xa, modv, norm_g[l], _pad_w_in(w_in[l]), tm=tm_in)
        qz, k, v = _attn_prep_call(p, cos, sin, ones, ones, P_AQ, P_AKV, False, False)
        y_a = _flash_w_call(a_sink[l], qz, k, v, p, P_AG)
        qz, k, vt = _attn_prep_call(p, cos, sin, jnp.tile(c_qn[l], 2)[None], jnp.tile(c_kn[l], 2)[None],
                                    P_CQ, P_CKV, True, True)
        y_c = _flash_t_call(qz, k, vt, p, P_CG)
        r, vb, kk, lw, a, kt, bonus = _rwkv_feat_call(p, b_mu[l], b_w0[l], b_wup[l], b_a0[l], b_aup[l],
                                                      b_kk[l], b_ka[l], b_rk[l])
        y_b2 = _rwkv_call(r, vb, kk, lw, a, kt)
        qd, kd, vd, gb = _gdn_feat_call(p, d_conv[l], d_alog[l], d_dtb[l])
        y_d2 = _gdn_call(qd, kd, vd, gb)
        acc = _branch_call(p, y_a, y_b2, bonus, y_c, y_d2, b_lng[l], b_lnb[l], d_norm[l],
                           g_up[l].astype(BF16), g_b[l], w_br[l].astype(BF16))
        xa = _outproj_call(xa, acc, w_out[l].astype(BF16), modv, final_g, l == L - 1)
    return xa
```
